```python
import jax, jax.numpy as jnp
from jax import lax
import numpy as np

D_MODEL = 2048
BATCH = 1
SEQ = 8192
DEPTH = 1
DEC_BATCH = 32
DEC_SEQ = 8
PAST_LEN = 16384
PAGE_SIZE = 128

HEAD_DIM = 128
SB_HEADS = 8
GDN_HEADS = 8
SB_WIDTH = SB_HEADS * HEAD_DIM
GDN_WIDTH = GDN_HEADS * HEAD_DIM
MIX_WIDTH = SB_WIDTH + GDN_WIDTH
CONV_W = 4
GDN_CHUNK = 64
Q_BLOCK = 128
N_EXPERTS = 64
TOP_K = 8
N_GROUPS = 8
TOPK_GROUPS = 4
D_EXPERT = 512
D_SHARED = 512
ROUTED_SCALE = 2.5
MOE_BLOCK = 128
NORM_EPS = 1e-6
SB_BIAS_NEAR = -2.0
SB_BIAS_FAR = -10.0
N_IN = 3 * SB_WIDTH + 4 * GDN_WIDTH + 2 * GDN_HEADS
IN_SPLITS = (SB_WIDTH, 2 * SB_WIDTH, 3 * SB_WIDTH, 3 * SB_WIDTH + 3 * GDN_WIDTH,
             3 * SB_WIDTH + 4 * GDN_WIDTH, 3 * SB_WIDTH + 4 * GDN_WIDTH + GDN_HEADS)

kernel_name = 'hymba_stickbreak_gdn_moe_step'


def rmsnorm(x, g):
    xf = x.astype(jnp.float32)
    y = xf * lax.rsqrt(jnp.mean(xf * xf, axis=-1, keepdims=True) + NORM_EPS)
    return (y * g.astype(jnp.float32)).astype(x.dtype)


def l2norm(x):
    return x * lax.rsqrt(jnp.sum(x * x, axis=-1, keepdims=True) + 1e-6)


def swiglu(h, wg, wu, wd):
    return (jax.nn.silu(h @ wg) * (h @ wu)) @ wd


def stick_breaking(q, k, v, q_pos, k_pos, bias):
    z = jnp.einsum('bqhd,bkhd->bhqk', q, k, preferred_element_type=jnp.float32) * (HEAD_DIM ** -0.5)
    z = z + bias.astype(jnp.float32)[None, :, None, None]
    causal = (k_pos[None, :] < q_pos[:, None])[None, None]
    log_1mb = jnp.where(causal, jax.nn.log_sigmoid(-z), 0.0)
    after = lax.cumsum(log_1mb, axis=3, reverse=True) - log_1mb
    w = jnp.where(causal, jnp.exp(jax.nn.log_sigmoid(z) + after), 0.0)
    return jnp.einsum('bhqk,bkhd->bqhd', w.astype(v.dtype), v)


def prompt_attend(q, k, v, bias):
    B, S = q.shape[0], q.shape[1]
    nq = S // Q_BLOCK
    pos = jnp.arange(S, dtype=jnp.int32)
    qb = jnp.moveaxis(q.reshape(B, nq, Q_BLOCK, SB_HEADS, HEAD_DIM), 1, 0)
    pb = pos.reshape(nq, Q_BLOCK)
    out = lax.map(lambda a: stick_breaking(a[0], k, v, a[1], pos, bias), (qb, pb))
    return jnp.moveaxis(out, 0, 1).reshape(B, S, SB_HEADS, HEAD_DIM)


def make_sample_attend(cache_k, cache_v, page_table, layer):
    def attend(q, k, v, bias):
        P = page_table.shape[1] * PAGE_SIZE
        DS = q.shape[1]
        k_pos = jnp.arange(P + DS, dtype=jnp.int32)
        q_pos = P + jnp.arange(DS, dtype=jnp.int32)
        def one(args):
            qb, kb, vb, pages = args
            kp = cache_k[layer, pages].reshape(P, SB_HEADS, HEAD_DIM)
            vp = cache_v[layer, pages].reshape(P, SB_HEADS, HEAD_DIM)
            kk = jnp.concatenate([kp.astype(kb.dtype), kb], axis=0)
            vv = jnp.concatenate([vp.astype(vb.dtype), vb], axis=0)
            return stick_breaking(qb[None], kk[None], vv[None], q_pos, k_pos, bias)[0]
        return lax.map(one, (q, k, v, page_table))
    return attend


def causal_conv(u, buf, w):
    L = u.shape[1]
    up = jnp.concatenate([buf.astype(u.dtype), u], axis=1)
    out = up[:, 0:L] * w[0]
    for i in range(1, CONV_W):
        out = out + up[:, i:i + L] * w[i]
    return out, up[:, L:]


def gated_delta_rule(q, k, v, beta, g, s0):
    B, L, H, DK = q.shape
    DV = v.shape[-1]
    C = GDN_CHUNK
    Lp = -(-L // C) * C
    pad = Lp - L
    def prep(t):
        t = jnp.pad(t, [(0, 0), (0, pad)] + [(0, 0)] * (t.ndim - 2))
        t = jnp.moveaxis(t, 2, 1)
        return t.reshape(t.shape[:2] + (Lp // C, C) + t.shape[3:])
    q, k, v, beta, g = prep(q), prep(k), prep(v), prep(beta), prep(g)
    gc = jnp.cumsum(g, axis=-1)
    idx = jnp.arange(C)
    strict = idx[:, None] > idx[None, :]
    incl = idx[:, None] >= idx[None, :]
    diff = gc[..., :, None] - gc[..., None, :]
    kb = k * beta[..., None]
    a_mat = jnp.einsum('bhntd,bhnid->bhnti', kb, k) * jnp.exp(jnp.where(strict, diff, -jnp.inf))
    rhs = jnp.concatenate([kb * jnp.exp(gc)[..., None], v * beta[..., None]], axis=-1)
    sol = lax.linalg.triangular_solve(a_mat, rhs, left_side=True, lower=True, unit_diagonal=True)
    w_mat, u_v = sol[..., :DK], sol[..., DK:]
    attn = jnp.einsum('bhntd,bhnid->bhnti', q, k) * jnp.exp(jnp.where(incl, diff, -jnp.inf))
    q_dec = q * jnp.exp(gc)[..., None]
    k_dec = k * jnp.exp(gc[..., -1:] - gc)[..., None]
    chunk_decay = jnp.exp(gc[..., -1])
    def step(s, xs):
        w_c, uv_c, att_c, qd_c, kd_c, dec_c = xs
        u = uv_c - jnp.einsum('bhtk,bhkv->bhtv', w_c, s)
        o = jnp.einsum('bhtk,bhkv->bhtv', qd_c, s) + jnp.einsum('bhti,bhiv->bhtv', att_c, u)
        s = dec_c[..., None, None] * s + jnp.einsum('bhik,bhiv->bhkv', kd_c, u)
        return s, o
    xs = tuple(jnp.moveaxis(t, 2, 0) for t in (w_mat, u_v, attn, q_dec, k_dec, chunk_decay))
    s_final, o = lax.scan(step, s0, xs)
    o = jnp.moveaxis(o, 0, 2).reshape(B, H, Lp, DV)[:, :, :L]
    return jnp.moveaxis(o, 1, 2), s_final


def token_mixing(h, conv_buf, s0, attend, w_in, conv_w, a_log, dt_bias, sb_bias, sb_g, gdn_g, w_out):
    B, L, _ = h.shape
    f32 = jnp.float32
    heads = lambda t: t.reshape(B, L, -1, HEAD_DIM)
    q_sb, k_sb, v_sb, qkv, zg, b_logit, a_logit = jnp.split(h @ w_in, list(IN_SPLITS), axis=-1)
    q_sb, k_sb, v_sb = heads(q_sb), heads(k_sb), heads(v_sb)
    o_sb = rmsnorm(attend(q_sb, k_sb, v_sb, sb_bias), sb_g)
    conv_out, new_buf = causal_conv(qkv, conv_buf, conv_w)
    conv_out = jax.nn.silu(conv_out.astype(f32))
    q_g, k_g, v_g = jnp.split(conv_out, 3, axis=-1)
    q_g = l2norm(heads(q_g)) * (HEAD_DIM ** -0.5)
    k_g = l2norm(heads(k_g))
    v_g = heads(v_g)
    beta = jax.nn.sigmoid(b_logit.astype(f32))
    g = -jnp.exp(a_log.astype(f32)) * jax.nn.softplus(a_logit.astype(f32) + dt_bias.astype(f32))
    o_g, s_new = gated_delta_rule(q_g, k_g, v_g, beta, g, s0.astype(f32))
    o_g = rmsnorm(o_g, gdn_g) * jax.nn.silu(heads(zg).astype(f32))
    o = jnp.concatenate([o_sb, o_g.astype(h.dtype)], axis=-2).reshape(B, L, MIX_WIDTH)
    return o @ w_out, k_sb, v_sb, new_buf, s_new.astype(s0.dtype)


def route(h, w_router, e_bias):
    T = h.shape[0]
    s = jax.nn.sigmoid(jnp.einsum('td,de->te', h, w_router, preferred_element_type=jnp.float32))
    sel = s + e_bias.astype(jnp.float32)
    grp_score = lax.top_k(sel.reshape(T, N_GROUPS, N_EXPERTS // N_GROUPS), 2)[0].sum(-1)
    _, gidx = lax.top_k(grp_score, TOPK_GROUPS)
    gmask = jax.nn.one_hot(gidx, N_GROUPS, dtype=jnp.float32).sum(-2) > 0
    emask = jnp.repeat(gmask, N_EXPERTS // N_GROUPS, axis=-1)
    _, eidx = lax.top_k(jnp.where(emask, sel, -jnp.inf), TOP_K)
    w = jnp.take_along_axis(s, eidx, axis=-1)
    w = w / jnp.sum(w, axis=-1, keepdims=True) * ROUTED_SCALE
    return eidx, w


def routed_experts(h, eidx, ew, w_gate, w_up, w_down):
    T, D = h.shape
    n = T * TOP_K
    flat_e = eidx.reshape(n)
    flat_w = ew.reshape(n)
    flat_t = jnp.repeat(jnp.arange(T, dtype=jnp.int32), TOP_K)
    order = jnp.argsort(flat_e, stable=True)
    se = flat_e[order]
    counts = jnp.bincount(flat_e, length=N_EXPERTS)
    pcounts = (counts + MOE_BLOCK - 1) // MOE_BLOCK * MOE_BLOCK
    pend = jnp.cumsum(pcounts)
    start = jnp.cumsum(counts) - counts
    dest = (pend - pcounts)[se] + jnp.arange(n, dtype=jnp.int32) - start[se]
    n_blocks = -(-(n + N_EXPERTS * (MOE_BLOCK - 1)) // MOE_BLOCK)
    slot_t = jnp.full((n_blocks * MOE_BLOCK,), T, jnp.int32).at[dest].set(flat_t[order])
    slot_w = jnp.zeros((n_blocks * MOE_BLOCK,), h.dtype).at[dest].set(flat_w[order].astype(h.dtype))
    blk_e = jnp.minimum(jnp.searchsorted(pend, jnp.arange(n_blocks, dtype=pend.dtype) * MOE_BLOCK, side='right'), N_EXPERTS - 1)
    hp = jnp.concatenate([h, jnp.zeros((1, D), h.dtype)], axis=0)
    def block(args):
        tok, e, gw = args
        return swiglu(hp[tok], w_gate[e], w_up[e], w_down[e]) * gw[:, None]
    yb = lax.map(block, (slot_t.reshape(n_blocks, MOE_BLOCK), blk_e, slot_w.reshape(n_blocks, MOE_BLOCK)))
    out = jnp.zeros((T + 1, D), h.dtype).at[slot_t].add(yb.reshape(-1, D))
    return out[:T]


def moe_ffn(h, w_router, e_bias, w_gate, w_up, w_down, ws_gate, ws_up, ws_down):
    eidx, ew = route(h, w_router, e_bias)
    return routed_experts(h, eidx, ew, w_gate, w_up, w_down) + swiglu(h, ws_gate, ws_up, ws_down)


def decoder_layer(x, c, conv_buf, s0, attend, p):
    B, L, D = x.shape
    mod = (jax.nn.silu(c) @ p['w_ada'] + p['b_ada']).reshape(B, 6, 1, D)
    sh1, sc1, g1, sh2, sc2, g2 = mod[:, 0], mod[:, 1], mod[:, 2], mod[:, 3], mod[:, 4], mod[:, 5]
    h = rmsnorm(x, p['pre_mix_g']) * (1 + sc1) + sh1
    o, k_sb, v_sb, new_buf, s_new = token_mixing(h, conv_buf, s0, attend, p['w_in'], p['conv_w'], p['a_log'],
                                                 p['dt_bias'], p['sb_bias'], p['sb_out_g'], p['gdn_out_g'],
                                                 p['w_out'])
    x = x + g1 * rmsnorm(o, p['post_mix_g'])
    h = rmsnorm(x, p['pre_ffn_g']) * (1 + sc2) + sh2
    f = moe_ffn(h.reshape(B * L, D), p['w_router'], p['e_bias'], p['w_gate'], p['w_up'], p['w_down'],
                p['ws_gate'], p['ws_up'], p['ws_down']).reshape(B, L, D)
    x = x + g2 * rmsnorm(f, p['post_ffn_g'])
    return x, k_sb, v_sb, new_buf, s_new


def setup_inputs(seed: int = 0) -> dict:
    key = jax.random.key(seed)
    kit = iter(jax.random.split(key, 40))
    def nrm(shape, scale):
        return jax.random.normal(next(kit), shape, jnp.float32) * scale
    n_pages = PAST_LEN // PAGE_SIZE
    n_used = DEC_BATCH * n_pages
    n_pool = n_used + max(1, n_used // 4)
    page_table = jax.random.permutation(next(kit), n_pool)[:n_used].reshape(DEC_BATCH, n_pages).astype(jnp.int32)
    x_prompt = nrm((BATCH, SEQ, D_MODEL), 1.0)
    x_sample = nrm((DEC_BATCH, DEC_SEQ, D_MODEL), 1.0)
    cache_sb_k = nrm((DEPTH, n_pool, PAGE_SIZE, SB_HEADS, HEAD_DIM), 1.0)
    cache_sb_v = nrm((DEPTH, n_pool, PAGE_SIZE, SB_HEADS, HEAD_DIM), 1.0)
    state_conv = nrm((DEPTH, DEC_BATCH, CONV_W - 1, 3 * GDN_WIDTH), 1.0)
    state_gdn = nrm((DEPTH, DEC_BATCH, GDN_HEADS, HEAD_DIM, HEAD_DIM), 0.5)
    c_prompt = nrm((BATCH, D_MODEL), 1.0)
    c_sample = nrm((DEC_BATCH, D_MODEL), 1.0)
    sb_bias = jnp.linspace(SB_BIAS_NEAR, SB_BIAS_FAR, SB_HEADS, dtype=jnp.float32)[None, :] + nrm((DEPTH, SB_HEADS), 0.1)
    return {
        'x_prompt': x_prompt, 'x_sample': x_sample,
        'cache_sb_k': cache_sb_k, 'cache_sb_v': cache_sb_v,
        'state_conv': state_conv, 'state_gdn': state_gdn,
        'page_table': page_table, 'c_prompt': c_prompt, 'c_sample': c_sample,
        'w_ada': nrm((DEPTH, D_MODEL, 6 * D_MODEL), D_MODEL ** -0.5),
        'b_ada': nrm((DEPTH, 6 * D_MODEL), 0.02),
        'pre_mix_g': 1.0 + nrm((DEPTH, D_MODEL), 0.05),
        'post_mix_g': 1.0 + nrm((DEPTH, D_MODEL), 0.05),
        'pre_ffn_g': 1.0 + nrm((DEPTH, D_MODEL), 0.05),
        'post_ffn_g': 1.0 + nrm((DEPTH, D_MODEL), 0.05),
        'w_in': nrm((DEPTH, D_MODEL, N_IN), D_MODEL ** -0.5),
        'conv_w': nrm((DEPTH, CONV_W, 3 * GDN_WIDTH), CONV_W ** -0.5),
        'a_log': jnp.log(jax.random.uniform(next(kit), (DEPTH, GDN_HEADS), jnp.float32, 1.0, 16.0)),
        'dt_bias': -4.0 + nrm((DEPTH, GDN_HEADS), 0.5),
        'sb_bias': sb_bias,
        'sb_out_g': 1.0 + nrm((DEPTH, HEAD_DIM), 0.05),
        'gdn_out_g': 1.0 + nrm((DEPTH, HEAD_DIM), 0.05),
        'w_out': nrm((DEPTH, MIX_WIDTH, D_MODEL), MIX_WIDTH ** -0.5),
        'w_router': nrm((DEPTH, D_MODEL, N_EXPERTS), D_MODEL ** -0.5),
        'e_bias': nrm((DEPTH, N_EXPERTS), 0.01),
        'w_gate': nrm((DEPTH, N_EXPERTS, D_MODEL, D_EXPERT), D_MODEL ** -0.5),
        'w_up': nrm((DEPTH, N_EXPERTS, D_MODEL, D_EXPERT), D_MODEL ** -0.5),
        'w_down': nrm((DEPTH, N_EXPERTS, D_EXPERT, D_MODEL), D_EXPERT ** -0.5),
        'ws_gate': nrm((DEPTH, D_MODEL, D_SHARED), D_MODEL ** -0.5),
        'ws_up': nrm((DEPTH, D_MODEL, D_SHARED), D_MODEL ** -0.5),
        'ws_down': nrm((DEPTH, D_SHARED, D_MODEL), D_SHARED ** -0.5),
    }


def reference(x_prompt, x_sample, cache_sb_k, cache_sb_v, state_conv, state_gdn, page_table, c_prompt, c_sample,
              w_ada, b_ada, pre_mix_g, post_mix_g, pre_ffn_g, post_ffn_g, w_in, conv_w, a_log, dt_bias, sb_bias,
              sb_out_g, gdn_out_g, w_out, w_router, e_bias, w_gate, w_up, w_down, ws_gate, ws_up, ws_down):
    y_p, y_s = x_prompt, x_sample
    Bp = x_prompt.shape[0]
    kp_l, vp_l, cp_l, sp_l, ks_l, vs_l, cs_l, ss_l = [], [], [], [], [], [], [], []
    for l in range(DEPTH):
        p = {'w_ada': w_ada[l], 'b_ada': b_ada[l], 'pre_mix_g': pre_mix_g[l], 'post_mix_g': post_mix_g[l],
             'pre_ffn_g': pre_ffn_g[l], 'post_ffn_g': post_ffn_g[l], 'w_in': w_in[l], 'conv_w': conv_w[l],
             'a_log': a_log[l], 'dt_bias': dt_bias[l], 'sb_bias': sb_bias[l], 'sb_out_g': sb_out_g[l],
             'gdn_out_g': gdn_out_g[l], 'w_out': w_out[l], 'w_router': w_router[l], 'e_bias': e_bias[l],
             'w_gate': w_gate[l], 'w_up': w_up[l], 'w_down': w_down[l], 'ws_gate': ws_gate[l],
             'ws_up': ws_up[l], 'ws_down': ws_down[l]}
        zero_buf = jnp.zeros((Bp, CONV_W - 1, 3 * GDN_WIDTH), x_prompt.dtype)
        zero_s = jnp.zeros((Bp, GDN_HEADS, HEAD_DIM, HEAD_DIM), state_gdn.dtype)
        y_p, kp, vp, cp, sp = decoder_layer(y_p, c_prompt, zero_buf, zero_s, prompt_attend, p)
        y_s, ks, vs, cs, ss = decoder_layer(y_s, c_sample, state_conv[l], state_gdn[l],
                                            make_sample_attend(cache_sb_k, cache_sb_v, page_table, l), p)
        kp_l.append(kp); vp_l.append(vp); cp_l.append(cp); sp_l.append(sp)
        ks_l.append(ks); vs_l.append(vs); cs_l.append(cs); ss_l.append(ss)
    k_prompt, v_prompt = jnp.stack(kp_l), jnp.stack(vp_l)
    conv_prompt, gdn_prompt = jnp.stack(cp_l), jnp.stack(sp_l)
    k_sample, v_sample = jnp.stack(ks_l), jnp.stack(vs_l)
    conv_sample, gdn_sample = jnp.stack(cs_l), jnp.stack(ss_l)
    return (y_p, y_s, k_prompt, v_prompt, conv_prompt, gdn_prompt, k_sample, v_sample, conv_sample, gdn_sample)
```

```python
import functools

import jax
import jax.numpy as jnp
from jax import lax
from jax.experimental import pallas as pl
from jax.experimental.pallas import tpu as pltpu

D_MODEL = 2048
HEAD_DIM = 128
SB_HEADS = 8
GDN_HEADS = 8
SB_WIDTH = SB_HEADS * HEAD_DIM
GDN_WIDTH = GDN_HEADS * HEAD_DIM
MIX_WIDTH = SB_WIDTH + GDN_WIDTH
CONV_W = 4
GDN_CHUNK = 64
Q_BLOCK = 128
PAGE_SIZE = 128
N_EXPERTS = 64
TOP_K = 8
N_GROUPS = 8
TOPK_GROUPS = 4
ROUTED_SCALE = 2.5
MOE_BLOCK = 128
NORM_EPS = 1e-6
IN_SPLITS = (SB_WIDTH, 2 * SB_WIDTH, 3 * SB_WIDTH, 3 * SB_WIDTH + 3 * GDN_WIDTH,
             3 * SB_WIDTH + 4 * GDN_WIDTH, 3 * SB_WIDTH + 4 * GDN_WIDTH + GDN_HEADS)

VMEM_LIMIT_BYTES = 48 * 1024 * 1024


def _mm_kernel(x_ref, w_ref, o_ref):
    o_ref[...] = jnp.dot(x_ref[...].astype(jnp.bfloat16), w_ref[...],
                         preferred_element_type=jnp.float32).astype(o_ref.dtype)


def _pick_tile(n, candidates):
    for c in candidates:
        if n % c == 0:
            return c
    return n


def matmul(x, w, out_dtype=jnp.float32):
    m, k = x.shape
    n = w.shape[1]
    pad_n = (-n) % 128
    wb = w.astype(jnp.bfloat16)
    if pad_n:
        wb = jnp.pad(wb, ((0, 0), (0, pad_n)))
    np_ = n + pad_n
    pad_m = (-m) % 8
    if pad_m:
        x = jnp.pad(x, ((0, pad_m), (0, 0)))
    mp = m + pad_m
    tm = _pick_tile(mp, (768, 512, 256, 128, 64, 32, 16, 8))
    tn = _pick_tile(np_, (1024, 512, 256, 128))
    out = pl.pallas_call(
        _mm_kernel,
        grid=(np_ // tn, mp // tm),
        in_specs=[pl.BlockSpec((tm, k), lambda j, i: (i, 0)),
                  pl.BlockSpec((k, tn), lambda j, i: (0, j))],
        out_specs=pl.BlockSpec((tm, tn), lambda j, i: (i, j)),
        out_shape=jax.ShapeDtypeStruct((mp, np_), out_dtype),
        compiler_params=pltpu.CompilerParams(
            dimension_semantics=("parallel", "parallel"),
            vmem_limit_bytes=VMEM_LIMIT_BYTES),
        name="matmul",
    )(x, wb)
    return out[:m, :n]


def rmsnorm(x, g):
    xf = x.astype(jnp.float32)
    y = xf * lax.rsqrt(jnp.mean(xf * xf, axis=-1, keepdims=True) + NORM_EPS)
    return (y * g.astype(jnp.float32)).astype(x.dtype)


def l2norm(x):
    return x * lax.rsqrt(jnp.sum(x * x, axis=-1, keepdims=True) + 1e-6)


def swiglu(h, wg, wu, wd):
    return matmul(jax.nn.silu(matmul(h, wg)) * matmul(h, wu), wd)


def stick_breaking(q, k, v, q_pos, k_pos, bias):
    z = jnp.einsum('bqhd,bkhd->bhqk', q, k, preferred_element_type=jnp.float32) * (HEAD_DIM ** -0.5)
    z = z + bias.astype(jnp.float32)[None, :, None, None]
    causal = (k_pos[None, :] < q_pos[:, None])[None, None]
    log_1mb = jnp.where(causal, jax.nn.log_sigmoid(-z), 0.0)
    after = lax.cumsum(log_1mb, axis=3, reverse=True) - log_1mb
    w = jnp.where(causal, jnp.exp(jax.nn.log_sigmoid(z) + after), 0.0)
    return jnp.einsum('bhqk,bkhd->bqhd', w.astype(v.dtype), v)


def prompt_attend(q, k, v, bias):
    B, S = q.shape[0], q.shape[1]
    nq = S // Q_BLOCK
    pos = jnp.arange(S, dtype=jnp.int32)
    qb = jnp.moveaxis(q.reshape(B, nq, Q_BLOCK, SB_HEADS, HEAD_DIM), 1, 0)
    pb = pos.reshape(nq, Q_BLOCK)
    out = lax.map(lambda a: stick_breaking(a[0], k, v, a[1], pos, bias), (qb, pb))
    return jnp.moveaxis(out, 0, 1).reshape(B, S, SB_HEADS, HEAD_DIM)


def make_sample_attend(cache_k, cache_v, page_table, layer):
    def attend(q, k, v, bias):
        P = page_table.shape[1] * PAGE_SIZE
        DS = q.shape[1]
        k_pos = jnp.arange(P + DS, dtype=jnp.int32)
        q_pos = P + jnp.arange(DS, dtype=jnp.int32)
        def one(args):
            qb, kb, vb, pages = args
            kp = cache_k[layer, pages].reshape(P, SB_HEADS, HEAD_DIM)
            vp = cache_v[layer, pages].reshape(P, SB_HEADS, HEAD_DIM)
            kk = jnp.concatenate([kp.astype(kb.dtype), kb], axis=0)
            vv = jnp.concatenate([vp.astype(vb.dtype), vb], axis=0)
            return stick_breaking(qb[None], kk[None], vv[None], q_pos, k_pos, bias)[0]
        return lax.map(one, (q, k, v, page_table))
    return attend


def causal_conv(u, buf, w):
    L = u.shape[1]
    up = jnp.concatenate([buf.astype(u.dtype), u], axis=1)
    out = up[:, 0:L] * w[0]
    for i in range(1, CONV_W):
        out = out + up[:, i:i + L] * w[i]
    return out, up[:, L:]


def gated_delta_rule(q, k, v, beta, g, s0):
    B, L, H, DK = q.shape
    DV = v.shape[-1]
    C = GDN_CHUNK
    Lp = -(-L // C) * C
    pad = Lp - L
    def prep(t):
        t = jnp.pad(t, [(0, 0), (0, pad)] + [(0, 0)] * (t.ndim - 2))
        t = jnp.moveaxis(t, 2, 1)
        return t.reshape(t.shape[:2] + (Lp // C, C) + t.shape[3:])
    q, k, v, beta, g = prep(q), prep(k), prep(v), prep(beta), prep(g)
    gc = jnp.cumsum(g, axis=-1)
    idx = jnp.arange(C)
    strict = idx[:, None] > idx[None, :]
    incl = idx[:, None] >= idx[None, :]
    diff = gc[..., :, None] - gc[..., None, :]
    kb = k * beta[..., None]
    a_mat = jnp.einsum('bhntd,bhnid->bhnti', kb, k) * jnp.exp(jnp.where(strict, diff, -jnp.inf))
    rhs = jnp.concatenate([kb * jnp.exp(gc)[..., None], v * beta[..., None]], axis=-1)
    sol = lax.linalg.triangular_solve(a_mat, rhs, left_side=True, lower=True, unit_diagonal=True)
    w_mat, u_v = sol[..., :DK], sol[..., DK:]
    attn = jnp.einsum('bhntd,bhnid->bhnti', q, k) * jnp.exp(jnp.where(incl, diff, -jnp.inf))
    q_dec = q * jnp.exp(gc)[..., None]
    k_dec = k * jnp.exp(gc[..., -1:] - gc)[..., None]
    chunk_decay = jnp.exp(gc[..., -1])
    def step(s, xs):
        w_c, uv_c, att_c, qd_c, kd_c, dec_c = xs
        u = uv_c - jnp.einsum('bhtk,bhkv->bhtv', w_c, s)
        o = jnp.einsum('bhtk,bhkv->bhtv', qd_c, s) + jnp.einsum('bhti,bhiv->bhtv', att_c, u)
        s = dec_c[..., None, None] * s + jnp.einsum('bhik,bhiv->bhkv', kd_c, u)
        return s, o
    xs = tuple(jnp.moveaxis(t, 2, 0) for t in (w_mat, u_v, attn, q_dec, k_dec, chunk_decay))
    s_final, o = lax.scan(step, s0, xs)
    o = jnp.moveaxis(o, 0, 2).reshape(B, H, Lp, DV)[:, :, :L]
    return jnp.moveaxis(o, 1, 2), s_final


def token_mixing(h, conv_buf, s0, attend, w_in, conv_w, a_log, dt_bias, sb_bias, sb_g, gdn_g, w_out):
    B, L, _ = h.shape
    f32 = jnp.float32
    heads = lambda t: t.reshape(B, L, -1, HEAD_DIM)
    proj = matmul(h.reshape(B * L, D_MODEL), w_in).reshape(B, L, -1)
    q_sb, k_sb, v_sb, qkv, zg, b_logit, a_logit = jnp.split(proj, list(IN_SPLITS), axis=-1)
    q_sb, k_sb, v_sb = heads(q_sb), heads(k_sb), heads(v_sb)
    o_sb = rmsnorm(attend(q_sb, k_sb, v_sb, sb_bias), sb_g)
    conv_out, new_buf = causal_conv(qkv, conv_buf, conv_w)
    conv_out = jax.nn.silu(conv_out.astype(f32))
    q_g, k_g, v_g = jnp.split(conv_out, 3, axis=-1)
    q_g = l2norm(heads(q_g)) * (HEAD_DIM ** -0.5)
    k_g = l2norm(heads(k_g))
    v_g = heads(v_g)
    beta = jax.nn.sigmoid(b_logit.astype(f32))
    g = -jnp.exp(a_log.astype(f32)) * jax.nn.softplus(a_logit.astype(f32) + dt_bias.astype(f32))
    o_g, s_new = gated_delta_rule(q_g, k_g, v_g, beta, g, s0.astype(f32))
    o_g = rmsnorm(o_g, gdn_g) * jax.nn.silu(heads(zg).astype(f32))
    o = jnp.concatenate([o_sb, o_g.astype(h.dtype)], axis=-2).reshape(B, L, MIX_WIDTH)
    return matmul(o.reshape(B * L, MIX_WIDTH), w_out).reshape(B, L, D_MODEL), k_sb, v_sb, new_buf, s_new.astype(s0.dtype)


def route(h, w_router, e_bias):
    T = h.shape[0]
    s = jax.nn.sigmoid(jnp.einsum('td,de->te', h, w_router, preferred_element_type=jnp.float32))
    sel = s + e_bias.astype(jnp.float32)
    grp_score = lax.top_k(sel.reshape(T, N_GROUPS, N_EXPERTS // N_GROUPS), 2)[0].sum(-1)
    _, gidx = lax.top_k(grp_score, TOPK_GROUPS)
    gmask = jax.nn.one_hot(gidx, N_GROUPS, dtype=jnp.float32).sum(-2) > 0
    emask = jnp.repeat(gmask, N_EXPERTS // N_GROUPS, axis=-1)
    _, eidx = lax.top_k(jnp.where(emask, sel, -jnp.inf), TOP_K)
    w = jnp.take_along_axis(s, eidx, axis=-1)
    w = w / jnp.sum(w, axis=-1, keepdims=True) * ROUTED_SCALE
    return eidx, w


def routed_experts(h, eidx, ew, w_gate, w_up, w_down):
    T, D = h.shape
    n = T * TOP_K
    flat_e = eidx.reshape(n)
    flat_w = ew.reshape(n)
    flat_t = jnp.repeat(jnp.arange(T, dtype=jnp.int32), TOP_K)
    order = jnp.argsort(flat_e, stable=True)
    se = flat_e[order]
    counts = jnp.bincount(flat_e, length=N_EXPERTS)
    pcounts = (counts + MOE_BLOCK - 1) // MOE_BLOCK * MOE_BLOCK
    pend = jnp.cumsum(pcounts)
    start = jnp.cumsum(counts) - counts
    dest = (pend - pcounts)[se] + jnp.arange(n, dtype=jnp.int32) - start[se]
    n_blocks = -(-(n + N_EXPERTS * (MOE_BLOCK - 1)) // MOE_BLOCK)
    slot_t = jnp.full((n_blocks * MOE_BLOCK,), T, jnp.int32).at[dest].set(flat_t[order])
    slot_w = jnp.zeros((n_blocks * MOE_BLOCK,), h.dtype).at[dest].set(flat_w[order].astype(h.dtype))
    blk_e = jnp.minimum(jnp.searchsorted(pend, jnp.arange(n_blocks, dtype=pend.dtype) * MOE_BLOCK, side='right'), N_EXPERTS - 1)
    hp = jnp.concatenate([h, jnp.zeros((1, D), h.dtype)], axis=0)
    def block(args):
        tok, e, gw = args
        x = hp[tok]
        return ((jax.nn.silu(x @ w_gate[e]) * (x @ w_up[e])) @ w_down[e]) * gw[:, None]
    yb = lax.map(block, (slot_t.reshape(n_blocks, MOE_BLOCK), blk_e, slot_w.reshape(n_blocks, MOE_BLOCK)))
    out = jnp.zeros((T + 1, D), h.dtype).at[slot_t].add(yb.reshape(-1, D))
    return out[:T]


def moe_ffn(h, w_router, e_bias, w_gate, w_up, w_down, ws_gate, ws_up, ws_down):
    eidx, ew = route(h, w_router, e_bias)
    return routed_experts(h, eidx, ew, w_gate, w_up, w_down) + swiglu(h, ws_gate, ws_up, ws_down)


def decoder_layer(x, c, conv_buf, s0, attend, p):
    B, L, D = x.shape
    mod = (jax.nn.silu(c) @ p['w_ada'] + p['b_ada']).reshape(B, 6, 1, D)
    sh1, sc1, g1, sh2, sc2, g2 = mod[:, 0], mod[:, 1], mod[:, 2], mod[:, 3], mod[:, 4], mod[:, 5]
    h = rmsnorm(x, p['pre_mix_g']) * (1 + sc1) + sh1
    o, k_sb, v_sb, new_buf, s_new = token_mixing(h, conv_buf, s0, attend, p['w_in'], p['conv_w'], p['a_log'],
                                                 p['dt_bias'], p['sb_bias'], p['sb_out_g'], p['gdn_out_g'],
                                                 p['w_out'])
    x = x + g1 * rmsnorm(o, p['post_mix_g'])
    h = rmsnorm(x, p['pre_ffn_g']) * (1 + sc2) + sh2
    f = moe_ffn(h.reshape(B * L, D), p['w_router'], p['e_bias'], p['w_gate'], p['w_up'], p['w_down'],
                p['ws_gate'], p['ws_up'], p['ws_down']).reshape(B, L, D)
    x = x + g2 * rmsnorm(f, p['post_ffn_g'])
    return x, k_sb, v_sb, new_buf, s_new


def kernel(x_prompt, x_sample, cache_sb_k, cache_sb_v, state_conv, state_gdn, page_table, c_prompt, c_sample,
           w_ada, b_ada, pre_mix_g, post_mix_g, pre_ffn_g, post_ffn_g, w_in, conv_w, a_log, dt_bias, sb_bias,
           sb_out_g, gdn_out_g, w_out, w_router, e_bias, w_gate, w_up, w_down, ws_gate, ws_up, ws_down):
    l = 0
    Bp = x_prompt.shape[0]
    p = {'w_ada': w_ada[l], 'b_ada': b_ada[l], 'pre_mix_g': pre_mix_g[l], 'post_mix_g': post_mix_g[l],
         'pre_ffn_g': pre_ffn_g[l], 'post_ffn_g': post_ffn_g[l], 'w_in': w_in[l], 'conv_w': conv_w[l],
         'a_log': a_log[l], 'dt_bias': dt_bias[l], 'sb_bias': sb_bias[l], 'sb_out_g': sb_out_g[l],
         'gdn_out_g': gdn_out_g[l], 'w_out': w_out[l], 'w_router': w_router[l], 'e_bias': e_bias[l],
         'w_gate': w_gate[l], 'w_up': w_up[l], 'w_down': w_down[l], 'ws_gate': ws_gate[l],
         'ws_up': ws_up[l], 'ws_down': ws_down[l]}
    zero_buf = jnp.zeros((Bp, CONV_W - 1, 3 * GDN_WIDTH), x_prompt.dtype)
    zero_s = jnp.zeros((Bp, GDN_HEADS, HEAD_DIM, HEAD_DIM), state_gdn.dtype)
    y_p, kp, vp, cp, sp = decoder_layer(x_prompt, c_prompt, zero_buf, zero_s, prompt_attend, p)
    y_s, ks, vs, cs, ss = decoder_layer(x_sample, c_sample, state_conv[l], state_gdn[l],
                                        make_sample_attend(cache_sb_k, cache_sb_v, page_table, l), p)
    st = lambda t: t[None]
    return (y_p, y_s, st(kp), st(vp), st(cp), st(sp), st(ks), st(vs), st(cs), st(ss))
```

```python
import functools

import jax
import jax.numpy as jnp
from jax import lax
from jax.experimental import pallas as pl
from jax.experimental.pallas import tpu as pltpu

D_MODEL = 2048
HEAD_DIM = 128
SB_HEADS = 8
GDN_HEADS = 8
SB_WIDTH = SB_HEADS * HEAD_DIM
GDN_WIDTH = GDN_HEADS * HEAD_DIM
CONV_W = 4
GDN_CHUNK = 64
N_EXPERTS = 64
TOP_K = 8
N_GROUPS = 8
TOPK_GROUPS = 4
ROUTED_SCALE = 2.5
NORM_EPS = 1e-6

VMEM_LIMIT_BYTES = 48 * 1024 * 1024
CUM_W = 256
PAGES_PER_STEP = 8
MOE_TM = 256
COMBINE_TB = 128


def _pick_tile(n, candidates):
    for c in candidates:
        if n % c == 0:
            return c
    return n


def _split_bf16(x):
    hi = x.astype(jnp.bfloat16)
    lo = (x - hi.astype(jnp.float32)).astype(jnp.bfloat16)
    return hi, lo


def _rms(x, g_row):
    return x * lax.rsqrt(jnp.mean(x * x, axis=-1, keepdims=True) + NORM_EPS) * g_row


def _proj_kernel(*refs, n_lhs, k_splits, pre_silu, has_bias, n_out):
    lhs = refs[:n_lhs]
    w_ref = refs[n_lhs]
    pos = n_lhs + 1
    b_ref = refs[pos] if has_bias else None
    pos += int(has_bias)
    outs = refs[pos:pos + n_out]
    wbf = refs[pos + n_out]

    @pl.when(pl.program_id(1) == 0)
    def _():
        wbf[...] = w_ref[...].astype(jnp.bfloat16)

    acc = None
    for x_ref, (k0, k1) in zip(lhs, k_splits):
        x = x_ref[...]
        if pre_silu:
            x = x * jax.nn.sigmoid(x)
        part = jnp.dot(x.astype(jnp.bfloat16), wbf[k0:k1, :], preferred_element_type=jnp.float32)
        acc = part if acc is None else acc + part
    if has_bias:
        acc = acc + b_ref[0:1, :]
    for o in outs:
        o[...] = acc.astype(o.dtype)


def project(lhs, w, col0, ncols, out_dtypes, *, bias=None, pre_silu=False):
    m = lhs[0].shape[0]
    k = w.shape[0]
    k_splits, k0 = [], 0
    for x in lhs:
        k_splits.append((k0, k0 + x.shape[1]))
        k0 += x.shape[1]
    assert k0 == k and m % 8 == 0
    np_ = -(-ncols // 128) * 128
    if np_ != ncols:
        assert bias is None
        w = jnp.pad(lax.slice_in_dim(w, col0, col0 + ncols, axis=1), ((0, 0), (0, np_ - ncols)))
        col0 = 0
    tn = _pick_tile(np_, (1024, 512, 256, 128))
    assert col0 % tn == 0
    tm = _pick_tile(m, (512, 256, 128, 64, 32, 16, 8))
    in_specs = [pl.BlockSpec((tm, x.shape[1]), lambda j, i: (i, 0)) for x in lhs]
    in_specs.append(pl.BlockSpec((k, tn), lambda j, i: (0, col0 // tn + j)))
    args = list(lhs) + [w]
    if bias is not None:
        b = jnp.broadcast_to(bias.astype(jnp.float32)[None, :], (8, bias.shape[0]))
        in_specs.append(pl.BlockSpec((8, tn), lambda j, i: (0, col0 // tn + j)))
        args.append(b)
    outs = pl.pallas_call(
        functools.partial(_proj_kernel, n_lhs=len(lhs), k_splits=tuple(k_splits), pre_silu=pre_silu,
                          has_bias=bias is not None, n_out=len(out_dtypes)),
        grid=(np_ // tn, m // tm),
        in_specs=in_specs,
        out_specs=[pl.BlockSpec((tm, tn), lambda j, i: (i, j)) for _ in out_dtypes],
        out_shape=[jax.ShapeDtypeStruct((m, np_), dt) for dt in out_dtypes],
        scratch_shapes=[pltpu.VMEM((k, tn), jnp.bfloat16)],
        compiler_params=pltpu.CompilerParams(dimension_semantics=("parallel", "arbitrary"),
                                             vmem_limit_bytes=VMEM_LIMIT_BYTES),
        name="project",
    )(*args)
    if np_ != ncols:
        outs = [o[:, :ncols] for o in outs]
    return outs


def _swiglu_up_kernel(x_ref, wg_ref, wu_ref, o_ref, wgb, wub):
    @pl.when(pl.program_id(0) == 0)
    def _():
        wgb[...] = wg_ref[...].astype(jnp.bfloat16)
        wub[...] = wu_ref[...].astype(jnp.bfloat16)
    x = x_ref[...].astype(jnp.bfloat16)
    g = jnp.dot(x, wgb[...], preferred_element_type=jnp.float32)
    u = jnp.dot(x, wub[...], preferred_element_type=jnp.float32)
    o_ref[...] = (g * jax.nn.sigmoid(g) * u).astype(o_ref.dtype)


def swiglu_up(x, wg, wu):
    m, k = x.shape
    f = wg.shape[1]
    tm = _pick_tile(m, (768, 512, 256, 128, 64, 32, 16, 8))
    return pl.pallas_call(
        _swiglu_up_kernel,
        grid=(m // tm,),
        in_specs=[pl.BlockSpec((tm, k), lambda i: (i, 0)),
                  pl.BlockSpec((k, f), lambda i: (0, 0)),
                  pl.BlockSpec((k, f), lambda i: (0, 0))],
        out_specs=pl.BlockSpec((tm, f), lambda i: (i, 0)),
        out_shape=jax.ShapeDtypeStruct((m, f), jnp.bfloat16),
        scratch_shapes=[pltpu.VMEM((k, f), jnp.bfloat16), pltpu.VMEM((k, f), jnp.bfloat16)],
        compiler_params=pltpu.CompilerParams(dimension_semantics=("arbitrary",),
                                             vmem_limit_bytes=VMEM_LIMIT_BYTES),
        name="swiglu_up",
    )(x, wg, wu)


def _mod_spec(a, tm):
    if a.shape[0] == 1:
        return pl.BlockSpec((1, a.shape[1]), lambda i: (0, 0))
    return pl.BlockSpec((tm, a.shape[1]), lambda i: (i, 0))


def _prenorm_kernel(x_ref, g_ref, sc_ref, sh_ref, o_ref):
    y = _rms(x_ref[...], g_ref[0:1, :]) * (1.0 + sc_ref[...]) + sh_ref[...]
    o_ref[...] = y.astype(o_ref.dtype)


def prenorm(x, g, sc, sh, out_dtype):
    m, d = x.shape
    tm = _pick_tile(m, (512, 256, 128, 64, 32, 16, 8))
    g8 = jnp.broadcast_to(g.astype(jnp.float32)[None, :], (8, d))
    return pl.pallas_call(
        _prenorm_kernel,
        grid=(m // tm,),
        in_specs=[pl.BlockSpec((tm, d), lambda i: (i, 0)), pl.BlockSpec((8, d), lambda i: (0, 0)),
                  _mod_spec(sc, tm), _mod_spec(sh, tm)],
        out_specs=pl.BlockSpec((tm, d), lambda i: (i, 0)),
        out_shape=jax.ShapeDtypeStruct((m, d), out_dtype),
        compiler_params=pltpu.CompilerParams(dimension_semantics=("parallel",),
                                             vmem_limit_bytes=VMEM_LIMIT_BYTES),
        name="prenorm",
    )(x, g8, sc, sh)


def _postmix_kernel(x_ref, o_ref_in, gate_ref, pg_ref, ng_ref, sc_ref, sh_ref, x1_ref, h2_ref):
    x1 = x_ref[...] + gate_ref[...] * _rms(o_ref_in[...], pg_ref[0:1, :])
    x1_ref[...] = x1
    h2_ref[...] = _rms(x1, ng_ref[0:1, :]) * (1.0 + sc_ref[...]) + sh_ref[...]


def postmix(x, o, gate, post_g, next_g, sc, sh):
    m, d = x.shape
    tm = _pick_tile(m, (256, 128, 64, 32, 16, 8))
    b8 = lambda v: jnp.broadcast_to(v.astype(jnp.float32)[None, :], (8, d))
    row = pl.BlockSpec((tm, d), lambda i: (i, 0))
    c8 = pl.BlockSpec((8, d), lambda i: (0, 0))
    return pl.pallas_call(
        _postmix_kernel,
        grid=(m // tm,),
        in_specs=[row, row, _mod_spec(gate, tm), c8, c8, _mod_spec(sc, tm), _mod_spec(sh, tm)],
        out_specs=[row, row],
        out_shape=[jax.ShapeDtypeStruct((m, d), jnp.float32), jax.ShapeDtypeStruct((m, d), jnp.float32)],
        compiler_params=pltpu.CompilerParams(dimension_semantics=("parallel",),
                                             vmem_limit_bytes=VMEM_LIMIT_BYTES),
        name="postmix",
    )(x, o, gate, b8(post_g), b8(next_g), sc, sh)


def _sb_tile(z, carry, v, u_mat, mask):
    sp = jnp.maximum(z, 0.0) + jnp.log1p(jnp.exp(-jnp.abs(z)))
    l1m = -sp
    if mask is not None:
        l1m = jnp.where(mask, l1m, 0.0)
    tk = z.shape[1]
    parts = []
    for c in reversed(range(tk // CUM_W)):
        blk = l1m[:, c * CUM_W:(c + 1) * CUM_W]
        hi, lo = _split_bf16(blk)
        after = (jnp.dot(hi, u_mat, preferred_element_type=jnp.float32)
                 + jnp.dot(lo, u_mat, preferred_element_type=jnp.float32)) + carry
        parts.append(after)
        carry = carry + jnp.sum(blk, axis=-1, keepdims=True)
    after = jnp.concatenate(parts[::-1], axis=1) if len(parts) > 1 else parts[0]
    w = jnp.exp(z - sp + after)
    if mask is not None:
        w = jnp.where(mask, w, 0.0)
    out = jnp.dot(w.astype(jnp.bfloat16), v, preferred_element_type=jnp.float32)
    return out, carry


def _cum_matrix():
    j = lax.broadcasted_iota(jnp.int32, (CUM_W, CUM_W), 0)
    s = lax.broadcasted_iota(jnp.int32, (CUM_W, CUM_W), 1)
    return (j > s).astype(jnp.bfloat16)


def _prompt_attn_kernel(q_ref, k_ref, v_ref, bias_ref, g_ref, u_ref, o_ref, *, tq, tk):
    qi = pl.program_id(1)
    q = q_ref[...]
    bias = bias_ref[0, 0:1, 0:1]
    u_mat = u_ref[...]
    scale = HEAD_DIM ** -0.5
    n_diag = tq // tk

    def logits(j):
        kt = k_ref[pl.ds(pl.multiple_of(j * tk, tk), tk), :]
        z = lax.dot_general(q, kt, (((1,), (1,)), ((), ())), preferred_element_type=jnp.float32)
        return z * scale + bias

    def values(j):
        return v_ref[pl.ds(pl.multiple_of(j * tk, tk), tk), :]

    acc = jnp.zeros((tq, HEAD_DIM), jnp.float32)
    carry = jnp.zeros((tq, 1), jnp.float32)
    row = lax.broadcasted_iota(jnp.int32, (tq, tk), 0)
    col = lax.broadcasted_iota(jnp.int32, (tq, tk), 1)
    for d in reversed(range(n_diag)):
        j = qi * n_diag + d
        mask = (col + d * tk) < row
        out, carry = _sb_tile(logits(j), carry, values(j), u_mat, mask)
        acc = acc + out

    def body(s, state):
        acc, carry = state
        j = qi * n_diag - 1 - s
        out, carry = _sb_tile(logits(j), carry, values(j), u_mat, None)
        return acc + out, carry

    acc, carry = lax.fori_loop(0, qi * n_diag, body, (acc, carry))
    o_ref[...] = _rms(acc, g_ref[0:1, :]).astype(o_ref.dtype)


def prompt_attention(q, k, v, sb_bias, sb_g, *, tq=512, tk=256):
    s, width = q.shape
    n_heads = width // HEAD_DIM
    bias = jnp.broadcast_to(sb_bias.astype(jnp.float32)[:, None, None], (n_heads, 8, 128))
    g = jnp.broadcast_to(sb_g.astype(jnp.float32)[None, :], (8, HEAD_DIM))
    return pl.pallas_call(
        functools.partial(_prompt_attn_kernel, tq=tq, tk=tk),
        grid=(n_heads, s // tq),
        in_specs=[pl.BlockSpec((tq, HEAD_DIM), lambda h, i: (i, h)),
                  pl.BlockSpec((s, HEAD_DIM), lambda h, i: (0, h)),
                  pl.BlockSpec((s, HEAD_DIM), lambda h, i: (0, h)),
                  pl.BlockSpec((1, 8, 128), lambda h, i: (h, 0, 0)),
                  pl.BlockSpec((8, HEAD_DIM), lambda h, i: (0, 0)),
                  pl.BlockSpec((CUM_W, CUM_W), lambda h, i: (0, 0))],
        out_specs=pl.BlockSpec((tq, HEAD_DIM), lambda h, i: (i, h)),
        out_shape=jax.ShapeDtypeStruct((s, width), jnp.bfloat16),
        compiler_params=pltpu.CompilerParams(dimension_semantics=("parallel", "arbitrary"),
                                             vmem_limit_bytes=VMEM_LIMIT_BYTES),
        name="prompt_attention",
    )(q, k, v, bias, g, _cum_matrix())


def _sample_attn_kernel(pt_ref, q_ref, knew_ref, vnew_ref, bias_ref, mask_ref, g_ref, u_ref, *rest,
                        n_pages_step, page, n_heads):
    kp_refs = rest[:n_pages_step]
    vp_refs = rest[n_pages_step:2 * n_pages_step]
    o_ref, kbf, vbf, acc_ref, carry_ref = rest[2 * n_pages_step:]
    c = pl.program_id(1)
    nc = pl.num_programs(1)
    q = q_ref[0]
    bias = bias_ref[...]
    u_mat = u_ref[...]
    scale = HEAD_DIM ** -0.5

    def tile(k_bf, v_bf, carry, mask):
        z = lax.dot_general(q, k_bf, (((1,), (1,)), ((), ())), preferred_element_type=jnp.float32)
        z = z * scale + bias
        return _sb_tile(z, carry, v_bf, u_mat, mask)

    @pl.when(c == 0)
    def _():
        carry0 = jnp.zeros((q.shape[0], 1), jnp.float32)
        out, carry = tile(knew_ref[0].astype(jnp.bfloat16), vnew_ref[0].astype(jnp.bfloat16), carry0,
                          mask_ref[...] > 0.5)
        acc_ref[...] = out
        carry_ref[...] = carry

    @pl.when(c > 0)
    def _():
        for i in range(n_pages_step):
            for h in range(n_heads):
                rows = pl.ds(h, page, stride=n_heads)
                cols = slice(h * HEAD_DIM, (h + 1) * HEAD_DIM)
                kbf[i * page:(i + 1) * page, cols] = kp_refs[i].at[0][rows, :].astype(jnp.bfloat16)
                vbf[i * page:(i + 1) * page, cols] = vp_refs[i].at[0][rows, :].astype(jnp.bfloat16)
        out, carry = tile(kbf[...], vbf[...], carry_ref[...], None)
        acc_ref[...] += out
        carry_ref[...] = carry

    @pl.when(c == nc - 1)
    def _():
        acc = acc_ref[...]
        nq = acc.shape[0] // n_heads
        for h in range(n_heads):
            o = acc[h * nq:(h + 1) * nq, h * HEAD_DIM:(h + 1) * HEAD_DIM]
            o_ref[0, :, h * HEAD_DIM:(h + 1) * HEAD_DIM] = _rms(o, g_ref[0:1, :]).astype(o_ref.dtype)


def sample_attention(q, k_new, v_new, cache_k, cache_v, page_table, sb_bias, sb_g):
    b, ds, width = q.shape
    n_heads = width // HEAD_DIM
    page = cache_k.shape[1] // n_heads
    n_pages = page_table.shape[1]
    pps = PAGES_PER_STEP
    assert n_pages % pps == 0 and (pps * page) % CUM_W == 0
    n_chunks = n_pages // pps
    rows = n_heads * ds
    qh = jnp.transpose(q.reshape(b, ds, n_heads, HEAD_DIM), (0, 2, 1, 3))
    eye = jnp.eye(n_heads, dtype=q.dtype)
    q_bd = (qh[:, :, :, None, :] * eye[None, :, None, :, None]).reshape(b, rows, width)
    pad = CUM_W - ds
    k_pad = jnp.pad(k_new, ((0, 0), (0, pad), (0, 0)))
    v_pad = jnp.pad(v_new, ((0, 0), (0, pad), (0, 0)))
    bias = jnp.repeat(sb_bias.astype(jnp.float32), ds)[:, None]
    qpos = jnp.tile(jnp.arange(ds), n_heads)
    mask = (jnp.arange(CUM_W)[None, :] < qpos[:, None]).astype(jnp.float32)
    g = jnp.broadcast_to(sb_g.astype(jnp.float32)[None, :], (8, HEAD_DIM))

    def page_map(i):
        def f(bi, c, pt):
            cc = jnp.maximum(c - 1, 0)
            return (pt[bi, n_pages - (cc + 1) * pps + i % pps], 0, 0)
        return f

    full = lambda shape: pl.BlockSpec(shape, lambda bi, c, pt: (0,) * len(shape))
    in_specs = [pl.BlockSpec((1, rows, width), lambda bi, c, pt: (bi, 0, 0)),
                pl.BlockSpec((1, CUM_W, width), lambda bi, c, pt: (bi, 0, 0)),
                pl.BlockSpec((1, CUM_W, width), lambda bi, c, pt: (bi, 0, 0)),
                full((rows, 1)), full((rows, CUM_W)), full((8, HEAD_DIM)), full((CUM_W, CUM_W))]
    in_specs += [pl.BlockSpec((1, page * n_heads, HEAD_DIM), page_map(i)) for i in range(2 * pps)]
    grid_spec = pltpu.PrefetchScalarGridSpec(
        num_scalar_prefetch=1, grid=(b, n_chunks + 1), in_specs=in_specs,
        out_specs=pl.BlockSpec((1, ds, width), lambda bi, c, pt: (bi, 0, 0)),
        scratch_shapes=[pltpu.VMEM((pps * page, width), jnp.bfloat16),
                        pltpu.VMEM((pps * page, width), jnp.bfloat16),
                        pltpu.VMEM((rows, width), jnp.float32),
                        pltpu.VMEM((rows, 1), jnp.float32)])
    return pl.pallas_call(
        functools.partial(_sample_attn_kernel, n_pages_step=pps, page=page, n_heads=n_heads),
        grid_spec=grid_spec,
        out_shape=jax.ShapeDtypeStruct((b, ds, width), jnp.bfloat16),
        compiler_params=pltpu.CompilerParams(dimension_semantics=("parallel", "arbitrary"),
                                             vmem_limit_bytes=VMEM_LIMIT_BYTES),
        name="sample_attention",
    )(page_table, q_bd, k_pad, v_pad, bias, mask, g, _cum_matrix(), *([cache_k] * pps), *([cache_v] * pps))


def _router_kernel(h_ref, wr_ref, eb_ref, idx_ref, w_ref):
    h_hi, h_lo = _split_bf16(h_ref[...])
    w_hi, w_lo = _split_bf16(wr_ref[...])
    dot = lambda a, b: jnp.dot(a, b, preferred_element_type=jnp.float32)
    logits = dot(h_hi, w_hi) + (dot(h_lo, w_hi) + dot(h_hi, w_lo))
    s = jax.nn.sigmoid(logits)
    sel = s + eb_ref[0:1, :]
    tm, ne = s.shape
    gsz = ne // N_GROUPS
    lane = lax.broadcasted_iota(jnp.int32, (tm, ne), 1)
    grp = lane // gsz
    neg = -jnp.inf

    def argmax_first(x):
        m = jnp.max(x, axis=-1, keepdims=True)
        i = jnp.min(jnp.where(x == m, lane, ne), axis=-1, keepdims=True)
        return m, i

    glane = lax.broadcasted_iota(jnp.int32, (tm, N_GROUPS), 1)
    gscore = jnp.zeros((tm, N_GROUPS), jnp.float32)
    for g in range(N_GROUPS):
        vals = jnp.where(grp == g, sel, neg)
        m1, i1 = argmax_first(vals)
        m2 = jnp.max(jnp.where(lane == i1, neg, vals), axis=-1, keepdims=True)
        gscore = jnp.where(glane == g, m1 + m2, gscore)
    emask = jnp.zeros((tm, ne), jnp.bool_)
    cur = gscore
    for _ in range(TOPK_GROUPS):
        m = jnp.max(cur, axis=-1, keepdims=True)
        gi = jnp.min(jnp.where(cur == m, glane, N_GROUPS), axis=-1, keepdims=True)
        emask = jnp.logical_or(emask, grp == gi)
        cur = jnp.where(glane == gi, neg, cur)
    olane = lax.broadcasted_iota(jnp.int32, idx_ref.shape, 1)
    idx_out = jnp.zeros(idx_ref.shape, jnp.int32)
    w_out = jnp.zeros(w_ref.shape, jnp.float32)
    cur = jnp.where(emask, sel, neg)
    for k in range(TOP_K):
        _, ei = argmax_first(cur)
        hit = lane == ei
        wk = jnp.sum(jnp.where(hit, s, 0.0), axis=-1, keepdims=True)
        idx_out = jnp.where(olane == k, ei, idx_out)
        w_out = jnp.where(olane == k, wk, w_out)
        cur = jnp.where(hit, neg, cur)
    tot = jnp.sum(w_out, axis=-1, keepdims=True)
    idx_ref[...] = idx_out
    w_ref[...] = w_out / tot * ROUTED_SCALE


def route(h, w_router, e_bias, *, tm=256):
    t, d = h.shape
    assert t % tm == 0
    ne = w_router.shape[1]
    eb = jnp.broadcast_to(e_bias.astype(jnp.float32)[None, :], (8, ne))
    idx, w = pl.pallas_call(
        _router_kernel,
        grid=(t // tm,),
        in_specs=[pl.BlockSpec((tm, d), lambda i: (i, 0)),
                  pl.BlockSpec((d, ne), lambda i: (0, 0)),
                  pl.BlockSpec((8, ne), lambda i: (0, 0))],
        out_specs=[pl.BlockSpec((tm, 128), lambda i: (i, 0)),
                   pl.BlockSpec((tm, 128), lambda i: (i, 0))],
        out_shape=[jax.ShapeDtypeStruct((t, 128), jnp.int32),
                   jax.ShapeDtypeStruct((t, 128), jnp.float32)],
        compiler_params=pltpu.CompilerParams(dimension_semantics=("parallel",),
                                             vmem_limit_bytes=VMEM_LIMIT_BYTES),
        name="moe_router",
    )(h, w_router, eb)
    return idx[:, :TOP_K], w[:, :TOP_K]


def _row_copy(src_hbm, dst, sem, r, i):
    return pltpu.make_async_copy(src_hbm.at[pl.ds(r, 1)], dst.at[pl.ds(i, 1)], sem)


def _gather_rows(src_hbm, idx_ref, dst, sem, n):
    def body(i, _):
        _row_copy(src_hbm, dst, sem, idx_ref[0, 0, i], i).start()
        return 0
    lax.fori_loop(0, n, body, 0)


def _wait_rows(src_hbm, dst, sem):
    pltpu.make_async_copy(src_hbm.at[pl.ds(0, dst.shape[0])], dst, sem).wait()


def _expert_kernel(nused_ref, blk_e_ref, h_hbm, idx_ref, idx_next_ref, wg_ref, wu_ref, wd_ref, y_ref,
                   xbuf, sems, *, tm):
    i = pl.program_id(0)
    n_used = nused_ref[0]
    slot = lax.rem(i, 2)

    @pl.when(i == 0)
    def _():
        _gather_rows(h_hbm, idx_ref, xbuf.at[0], sems.at[0], tm)

    @pl.when(i + 1 < n_used)
    def _():
        _gather_rows(h_hbm, idx_next_ref, xbuf.at[1 - slot], sems.at[1 - slot], tm)

    @pl.when(i < n_used)
    def _():
        _wait_rows(h_hbm, xbuf.at[slot], sems.at[slot])
        x = xbuf[slot].astype(jnp.bfloat16)
        g = jnp.dot(x, wg_ref[0], preferred_element_type=jnp.float32)
        u = jnp.dot(x, wu_ref[0], preferred_element_type=jnp.float32)
        a = (g * jax.nn.sigmoid(g) * u).astype(jnp.bfloat16)
        y_ref[...] = jnp.dot(a, wd_ref[0], preferred_element_type=jnp.float32)

    @pl.when(i >= n_used)
    def _():
        y_ref[...] = jnp.zeros_like(y_ref)


def expert_ffn(h, slot_t, blk_e, n_used, w_gate, w_up, w_down, *, tm=MOE_TM):
    t, d = h.shape
    _, _, f = w_gate.shape
    n_blocks = blk_e.shape[0]
    idx3 = slot_t.reshape(n_blocks, 1, tm)
    grid_spec = pltpu.PrefetchScalarGridSpec(
        num_scalar_prefetch=2, grid=(n_blocks,),
        in_specs=[pl.BlockSpec(memory_space=pl.ANY),
                  pl.BlockSpec((1, 1, tm), lambda i, nu, be: (i, 0, 0), memory_space=pltpu.SMEM),
                  pl.BlockSpec((1, 1, tm), lambda i, nu, be: (jnp.minimum(i + 1, n_blocks - 1), 0, 0),
                               memory_space=pltpu.SMEM),
                  pl.BlockSpec((1, d, f), lambda i, nu, be: (be[i], 0, 0)),
                  pl.BlockSpec((1, d, f), lambda i, nu, be: (be[i], 0, 0)),
                  pl.BlockSpec((1, f, d), lambda i, nu, be: (be[i], 0, 0))],
        out_specs=pl.BlockSpec((tm, d), lambda i, nu, be: (i, 0)),
        scratch_shapes=[pltpu.VMEM((2, tm, d), jnp.float32), pltpu.SemaphoreType.DMA((2,))])
    return pl.pallas_call(
        functools.partial(_expert_kernel, tm=tm),
        grid_spec=grid_spec,
        out_shape=jax.ShapeDtypeStruct((n_blocks * tm, d), jnp.float32),
        compiler_params=pltpu.CompilerParams(dimension_semantics=("arbitrary",),
                                             vmem_limit_bytes=VMEM_LIMIT_BYTES),
        name="moe_experts",
    )(n_used, blk_e, h, idx3, idx3, w_gate, w_up, w_down)


def _combine_kernel(y_hbm, pos_ref, pos_next_ref, w_ref, sh_ref, x_ref, gate_ref, g_ref, o_ref,
                    ybuf, sems, *, tb):
    i = pl.program_id(0)
    n = pl.num_programs(0)
    slot = lax.rem(i, 2)
    rows = tb * TOP_K

    @pl.when(i == 0)
    def _():
        _gather_rows(y_hbm, pos_ref, ybuf.at[0], sems.at[0], rows)

    @pl.when(i + 1 < n)
    def _():
        _gather_rows(y_hbm, pos_next_ref, ybuf.at[1 - slot], sems.at[1 - slot], rows)

    _wait_rows(y_hbm, ybuf.at[slot], sems.at[slot])
    w = w_ref[...]
    f = sh_ref[...]
    for k in range(TOP_K):
        f = f + w[:, k:k + 1] * ybuf[slot, k * tb:(k + 1) * tb, :]
    o_ref[...] = x_ref[...] + gate_ref[...] * _rms(f, g_ref[0:1, :])


def moe_combine(y_sorted, pos, w, shared, x, gate, post_g, *, tb=COMBINE_TB):
    t, d = x.shape
    assert t % tb == 0
    nb = t // tb
    pos3 = jnp.transpose(pos.reshape(nb, tb, TOP_K), (0, 2, 1)).reshape(nb, 1, tb * TOP_K)
    g = jnp.broadcast_to(post_g.astype(jnp.float32)[None, :], (8, d))
    row = lambda i: (i, 0)
    return pl.pallas_call(
        functools.partial(_combine_kernel, tb=tb),
        grid=(nb,),
        in_specs=[pl.BlockSpec(memory_space=pl.ANY),
                  pl.BlockSpec((1, 1, tb * TOP_K), lambda i: (i, 0, 0), memory_space=pltpu.SMEM),
                  pl.BlockSpec((1, 1, tb * TOP_K), lambda i: (jnp.minimum(i + 1, nb - 1), 0, 0),
                               memory_space=pltpu.SMEM),
                  pl.BlockSpec((tb, TOP_K), row),
                  pl.BlockSpec((tb, d), row), pl.BlockSpec((tb, d), row), _mod_spec(gate, tb),
                  pl.BlockSpec((8, d), lambda i: (0, 0))],
        out_specs=pl.BlockSpec((tb, d), row),
        out_shape=jax.ShapeDtypeStruct((t, d), jnp.float32),
        scratch_shapes=[pltpu.VMEM((2, tb * TOP_K, d), jnp.float32), pltpu.SemaphoreType.DMA((2,))],
        compiler_params=pltpu.CompilerParams(dimension_semantics=("arbitrary",),
                                             vmem_limit_bytes=VMEM_LIMIT_BYTES),
        name="moe_combine",
    )(y_sorted, pos3, pos3, w, shared, x, gate, g)


def dispatch_plan(eidx, *, tm=MOE_TM):
    t = eidx.shape[0]
    n = t * TOP_K
    flat_e = eidx.reshape(n)
    flat_t = jnp.repeat(jnp.arange(t, dtype=jnp.int32), TOP_K)
    order = jnp.argsort(flat_e, stable=True)
    se = flat_e[order]
    counts = jnp.bincount(flat_e, length=N_EXPERTS).astype(jnp.int32)
    pcounts = (counts + tm - 1) // tm * tm
    pend = jnp.cumsum(pcounts)
    start = jnp.cumsum(counts) - counts
    dest = (pend - pcounts)[se] + jnp.arange(n, dtype=jnp.int32) - start[se]
    n_blocks = -(-(n + N_EXPERTS * (tm - 1)) // tm)
    slot_t = jnp.zeros((n_blocks * tm,), jnp.int32).at[dest].set(flat_t[order])
    pos = jnp.zeros((n,), jnp.int32).at[order].set(dest).reshape(t, TOP_K)
    blk_e = jnp.minimum(jnp.searchsorted(pend, jnp.arange(n_blocks, dtype=pend.dtype) * tm, side='right'),
                        N_EXPERTS - 1).astype(jnp.int32)
    n_used = (pend[-1] // tm).astype(jnp.int32).reshape(1)
    return slot_t, pos, blk_e, n_used


def l2norm(x):
    return x * lax.rsqrt(jnp.sum(x * x, axis=-1, keepdims=True) + 1e-6)


def rmsnorm(x, g):
    xf = x.astype(jnp.float32)
    y = xf * lax.rsqrt(jnp.mean(xf * xf, axis=-1, keepdims=True) + NORM_EPS)
    return (y * g.astype(jnp.float32)).astype(x.dtype)


def causal_conv(u, buf, w):
    L = u.shape[1]
    up = jnp.concatenate([buf.astype(u.dtype), u], axis=1)
    out = up[:, 0:L] * w[0]
    for i in range(1, CONV_W):
        out = out + up[:, i:i + L] * w[i]
    return out, up[:, L:]


def gated_delta_rule(q, k, v, beta, g, s0):
    B, L, H, DK = q.shape
    DV = v.shape[-1]
    C = GDN_CHUNK
    Lp = -(-L // C) * C
    pad = Lp - L
    def prep(t):
        t = jnp.pad(t, [(0, 0), (0, pad)] + [(0, 0)] * (t.ndim - 2))
        t = jnp.moveaxis(t, 2, 1)
        return t.reshape(t.shape[:2] + (Lp // C, C) + t.shape[3:])
    q, k, v, beta, g = prep(q), prep(k), prep(v), prep(beta), prep(g)
    gc = jnp.cumsum(g, axis=-1)
    idx = jnp.arange(C)
    strict = idx[:, None] > idx[None, :]
    incl = idx[:, None] >= idx[None, :]
    diff = gc[..., :, None] - gc[..., None, :]
    kb = k * beta[..., None]
    a_mat = jnp.einsum('bhntd,bhnid->bhnti', kb, k) * jnp.exp(jnp.where(strict, diff, -jnp.inf))
    rhs = jnp.concatenate([kb * jnp.exp(gc)[..., None], v * beta[..., None]], axis=-1)
    sol = lax.linalg.triangular_solve(a_mat, rhs, left_side=True, lower=True, unit_diagonal=True)
    w_mat, u_v = sol[..., :DK], sol[..., DK:]
    attn = jnp.einsum('bhntd,bhnid->bhnti', q, k) * jnp.exp(jnp.where(incl, diff, -jnp.inf))
    q_dec = q * jnp.exp(gc)[..., None]
    k_dec = k * jnp.exp(gc[..., -1:] - gc)[..., None]
    chunk_decay = jnp.exp(gc[..., -1])
    def step(s, xs):
        w_c, uv_c, att_c, qd_c, kd_c, dec_c = xs
        u = uv_c - jnp.einsum('bhtk,bhkv->bhtv', w_c, s)
        o = jnp.einsum('bhtk,bhkv->bhtv', qd_c, s) + jnp.einsum('bhti,bhiv->bhtv', att_c, u)
        s = dec_c[..., None, None] * s + jnp.einsum('bhik,bhiv->bhkv', kd_c, u)
        return s, o
    xs = tuple(jnp.moveaxis(t, 2, 0) for t in (w_mat, u_v, attn, q_dec, k_dec, chunk_decay))
    s_final, o = lax.scan(step, s0, xs)
    o = jnp.moveaxis(o, 0, 2).reshape(B, H, Lp, DV)[:, :, :L]
    return jnp.moveaxis(o, 1, 2), s_final


def gdn_mixer(qkv, zg, b_logit, a_logit, conv_buf, s0, conv_w, a_log, dt_bias, gdn_g):
    B, L, _ = qkv.shape
    f32 = jnp.float32
    heads = lambda t: t.reshape(B, L, -1, HEAD_DIM)
    conv_out, new_buf = causal_conv(qkv, conv_buf, conv_w)
    conv_out = jax.nn.silu(conv_out.astype(f32))
    q_g, k_g, v_g = jnp.split(conv_out, 3, axis=-1)
    q_g = l2norm(heads(q_g)) * (HEAD_DIM ** -0.5)
    k_g = l2norm(heads(k_g))
    v_g = heads(v_g)
    beta = jax.nn.sigmoid(b_logit.astype(f32))
    g = -jnp.exp(a_log.astype(f32)) * jax.nn.softplus(a_logit.astype(f32) + dt_bias.astype(f32))
    o_g, s_new = gated_delta_rule(q_g, k_g, v_g, beta, g, s0.astype(f32))
    o_g = rmsnorm(o_g, gdn_g) * jax.nn.silu(heads(zg).astype(f32))
    return o_g.reshape(B * L, GDN_WIDTH).astype(jnp.bfloat16), new_buf, s_new.astype(s0.dtype)


def _expand(mod_rows, reps):
    return mod_rows if reps is None else jnp.repeat(mod_rows, reps, axis=0)


def kernel(x_prompt, x_sample, cache_sb_k, cache_sb_v, state_conv, state_gdn, page_table, c_prompt, c_sample,
           w_ada, b_ada, pre_mix_g, post_mix_g, pre_ffn_g, post_ffn_g, w_in, conv_w, a_log, dt_bias, sb_bias,
           sb_out_g, gdn_out_g, w_out, w_router, e_bias, w_gate, w_up, w_down, ws_gate, ws_up, ws_down):
    l = 0
    bp, sp, d = x_prompt.shape
    bs, ss, _ = x_sample.shape
    assert bp == 1
    f32, bf16 = jnp.float32, jnp.bfloat16

    c_all = jnp.concatenate([c_prompt, c_sample], axis=0)
    c_pad = jnp.pad(c_all, ((0, (-c_all.shape[0]) % 8), (0, 0)))
    mod, = project([c_pad], w_ada[l], 0, 6 * d, (f32,), bias=b_ada[l], pre_silu=True)
    mod = mod[:bp + bs].reshape(bp + bs, 6, d)
    mod_p = [mod[:bp, i] for i in range(6)]
    mod_s = [_expand(mod[bp:, i], ss) for i in range(6)]

    groups = (
        dict(x=x_prompt.reshape(sp, d), mod=mod_p, batch=bp, length=sp),
        dict(x=x_sample.reshape(bs * ss, d), mod=mod_s, batch=bs, length=ss),
    )
    n_cache = cache_sb_k.shape[1]
    cache_k = cache_sb_k[l].reshape(n_cache, cache_sb_k.shape[2] * SB_HEADS, HEAD_DIM)
    cache_v = cache_sb_v[l].reshape(n_cache, cache_sb_v.shape[2] * SB_HEADS, HEAD_DIM)
    conv_bufs = (jnp.zeros((bp, CONV_W - 1, 3 * GDN_WIDTH), f32), state_conv[l])
    states = (jnp.zeros((bp, GDN_HEADS, HEAD_DIM, HEAD_DIM), state_gdn.dtype), state_gdn[l])

    results = []
    for gi, grp in enumerate(groups):
        x, (sh1, sc1, g1, sh2, sc2, g2) = grp['x'], grp['mod']
        b, length = grp['batch'], grp['length']
        h = prenorm(x, pre_mix_g[l], sc1, sh1, bf16)
        wi = w_in[l]
        q_sb, = project([h], wi, 0, SB_WIDTH, (bf16,))
        k_sb, k_bf = project([h], wi, SB_WIDTH, SB_WIDTH, (f32, bf16))
        v_sb, v_bf = project([h], wi, 2 * SB_WIDTH, SB_WIDTH, (f32, bf16))
        qkv, = project([h], wi, 3 * SB_WIDTH, 3 * GDN_WIDTH, (f32,))
        zg, = project([h], wi, 3 * SB_WIDTH + 3 * GDN_WIDTH, GDN_WIDTH, (f32,))
        ab, = project([h], wi, 3 * SB_WIDTH + 4 * GDN_WIDTH, 2 * GDN_HEADS, (f32,))
        if gi == 0:
            o_sb = prompt_attention(q_sb, k_bf, v_bf, sb_bias[l], sb_out_g[l])
        else:
            o_sb = sample_attention(q_sb.reshape(b, length, SB_WIDTH), k_sb.reshape(b, length, SB_WIDTH),
                                    v_sb.reshape(b, length, SB_WIDTH), cache_k, cache_v, page_table,
                                    sb_bias[l], sb_out_g[l]).reshape(b * length, SB_WIDTH)
        o_g, new_buf, s_new = gdn_mixer(qkv.reshape(b, length, -1), zg.reshape(b, length, -1),
                                        ab[:, :GDN_HEADS].reshape(b, length, -1),
                                        ab[:, GDN_HEADS:].reshape(b, length, -1),
                                        conv_bufs[gi], states[gi], conv_w[l], a_log[l], dt_bias[l], gdn_out_g[l])
        o, = project([o_sb, o_g], w_out[l], 0, d, (f32,))
        x1, h2 = postmix(x, o, g1, post_mix_g[l], pre_ffn_g[l], sc2, sh2)
        results.append(dict(x1=x1, h2=h2, g2=g2, k=k_sb.reshape(b, length, SB_HEADS, HEAD_DIM),
                            v=v_sb.reshape(b, length, SB_HEADS, HEAD_DIM), conv=new_buf, state=s_new))

    h2 = jnp.concatenate([r['h2'] for r in results], axis=0)
    eidx, ew = route(h2, w_router[l], e_bias[l])
    slot_t, pos, blk_e, n_used = dispatch_plan(eidx)
    y_sorted = expert_ffn(h2, slot_t, blk_e, n_used, w_gate[l].astype(bf16), w_up[l].astype(bf16),
                          w_down[l].astype(bf16))
    act = swiglu_up(h2, ws_gate[l], ws_up[l])
    shared, = project([act], ws_down[l], 0, d, (f32,))
    outs, t0 = [], 0
    for r in results:
        t1 = t0 + r['x1'].shape[0]
        outs.append(moe_combine(y_sorted, pos[t0:t1], ew[t0:t1], shared[t0:t1], r['x1'], r['g2'], post_ffn_g[l]))
        t0 = t1

    y_p = outs[0].reshape(bp, sp, d)
    y_s = outs[1].reshape(bs, ss, d)
    st = lambda t: t[None]
    rp, rs = results
    return (y_p, y_s, st(rp['k']), st(rp['v']), st(rp['conv']), st(rp['state']),
            st(rs['k']), st(rs['v']), st(rs['conv']), st(rs['state']))
```

```python
import functools

import jax
import jax.numpy as jnp
from jax import lax
from jax.experimental import pallas as pl
from jax.experimental.pallas import tpu as pltpu

D_MODEL = 2048
HEAD_DIM = 128
SB_HEADS = 8
GDN_HEADS = 8
SB_WIDTH = SB_HEADS * HEAD_DIM
GDN_WIDTH = GDN_HEADS * HEAD_DIM
CONV_W = 4
GDN_CHUNK = 64
N_EXPERTS = 64
TOP_K = 8
N_GROUPS = 8
TOPK_GROUPS = 4
ROUTED_SCALE = 2.5
NORM_EPS = 1e-6

VMEM_LIMIT_BYTES = 48 * 1024 * 1024
CUM_W = 256
PAGES_PER_STEP = 8
MOE_TM = 256
COMBINE_TB = 128


def _pick_tile(n, candidates):
    for c in candidates:
        if n % c == 0:
            return c
    return n


def _split_bf16(x):
    hi = x.astype(jnp.bfloat16)
    lo = (x - hi.astype(jnp.float32)).astype(jnp.bfloat16)
    return hi, lo


def _rms(x, g_row):
    return x * lax.rsqrt(jnp.mean(x * x, axis=-1, keepdims=True) + NORM_EPS) * g_row


def _proj_kernel(*refs, n_lhs, k_splits, pre_silu, has_bias, n_out):
    lhs = refs[:n_lhs]
    w_ref = refs[n_lhs]
    pos = n_lhs + 1
    b_ref = refs[pos] if has_bias else None
    pos += int(has_bias)
    outs = refs[pos:pos + n_out]
    wbf = refs[pos + n_out]

    @pl.when(pl.program_id(1) == 0)
    def _():
        wbf[...] = w_ref[...].astype(jnp.bfloat16)

    acc = None
    for x_ref, (k0, k1) in zip(lhs, k_splits):
        x = x_ref[...]
        if pre_silu:
            x = x * jax.nn.sigmoid(x)
        part = jnp.dot(x.astype(jnp.bfloat16), wbf[k0:k1, :], preferred_element_type=jnp.float32)
        acc = part if acc is None else acc + part
    if has_bias:
        acc = acc + b_ref[0:1, :]
    for o in outs:
        o[...] = acc.astype(o.dtype)


def project(lhs, w, col0, ncols, out_dtypes, *, bias=None, pre_silu=False):
    m = lhs[0].shape[0]
    k = w.shape[0]
    k_splits, k0 = [], 0
    for x in lhs:
        k_splits.append((k0, k0 + x.shape[1]))
        k0 += x.shape[1]
    assert k0 == k and m % 8 == 0
    np_ = -(-ncols // 128) * 128
    if np_ != ncols:
        assert bias is None
        w = jnp.pad(lax.slice_in_dim(w, col0, col0 + ncols, axis=1), ((0, 0), (0, np_ - ncols)))
        col0 = 0
    tn = _pick_tile(np_, (1024, 512, 256, 128))
    assert col0 % tn == 0
    tm = _pick_tile(m, (512, 256, 128, 64, 32, 16, 8))
    in_specs = [pl.BlockSpec((tm, x.shape[1]), lambda j, i: (i, 0)) for x in lhs]
    in_specs.append(pl.BlockSpec((k, tn), lambda j, i: (0, col0 // tn + j)))
    args = list(lhs) + [w]
    if bias is not None:
        b = jnp.broadcast_to(bias.astype(jnp.float32)[None, :], (8, bias.shape[0]))
        in_specs.append(pl.BlockSpec((8, tn), lambda j, i: (0, col0 // tn + j)))
        args.append(b)
    outs = pl.pallas_call(
        functools.partial(_proj_kernel, n_lhs=len(lhs), k_splits=tuple(k_splits), pre_silu=pre_silu,
                          has_bias=bias is not None, n_out=len(out_dtypes)),
        grid=(np_ // tn, m // tm),
        in_specs=in_specs,
        out_specs=[pl.BlockSpec((tm, tn), lambda j, i: (i, j)) for _ in out_dtypes],
        out_shape=[jax.ShapeDtypeStruct((m, np_), dt) for dt in out_dtypes],
        scratch_shapes=[pltpu.VMEM((k, tn), jnp.bfloat16)],
        compiler_params=pltpu.CompilerParams(dimension_semantics=("parallel", "arbitrary"),
                                             vmem_limit_bytes=VMEM_LIMIT_BYTES),
        name="project",
    )(*args)
    if np_ != ncols:
        outs = [o[:, :ncols] for o in outs]
    return outs


def _swiglu_up_kernel(x_ref, wg_ref, wu_ref, o_ref, wgb, wub):
    @pl.when(pl.program_id(0) == 0)
    def _():
        wgb[...] = wg_ref[...].astype(jnp.bfloat16)
        wub[...] = wu_ref[...].astype(jnp.bfloat16)
    x = x_ref[...].astype(jnp.bfloat16)
    g = jnp.dot(x, wgb[...], preferred_element_type=jnp.float32)
    u = jnp.dot(x, wub[...], preferred_element_type=jnp.float32)
    o_ref[...] = (g * jax.nn.sigmoid(g) * u).astype(o_ref.dtype)


def swiglu_up(x, wg, wu):
    m, k = x.shape
    f = wg.shape[1]
    tm = _pick_tile(m, (768, 512, 256, 128, 64, 32, 16, 8))
    return pl.pallas_call(
        _swiglu_up_kernel,
        grid=(m // tm,),
        in_specs=[pl.BlockSpec((tm, k), lambda i: (i, 0)),
                  pl.BlockSpec((k, f), lambda i: (0, 0)),
                  pl.BlockSpec((k, f), lambda i: (0, 0))],
        out_specs=pl.BlockSpec((tm, f), lambda i: (i, 0)),
        out_shape=jax.ShapeDtypeStruct((m, f), jnp.bfloat16),
        scratch_shapes=[pltpu.VMEM((k, f), jnp.bfloat16), pltpu.VMEM((k, f), jnp.bfloat16)],
        compiler_params=pltpu.CompilerParams(dimension_semantics=("arbitrary",),
                                             vmem_limit_bytes=VMEM_LIMIT_BYTES),
        name="swiglu_up",
    )(x, wg, wu)


def _mod_spec(a, tm):
    if a.shape[0] == 1:
        return pl.BlockSpec((1, a.shape[1]), lambda i: (0, 0))
    return pl.BlockSpec((tm, a.shape[1]), lambda i: (i, 0))


def _prenorm_kernel(x_ref, g_ref, sc_ref, sh_ref, o_ref):
    y = _rms(x_ref[...], g_ref[0:1, :]) * (1.0 + sc_ref[...]) + sh_ref[...]
    o_ref[...] = y.astype(o_ref.dtype)


def prenorm(x, g, sc, sh, out_dtype):
    m, d = x.shape
    tm = _pick_tile(m, (512, 256, 128, 64, 32, 16, 8))
    g8 = jnp.broadcast_to(g.astype(jnp.float32)[None, :], (8, d))
    return pl.pallas_call(
        _prenorm_kernel,
        grid=(m // tm,),
        in_specs=[pl.BlockSpec((tm, d), lambda i: (i, 0)), pl.BlockSpec((8, d), lambda i: (0, 0)),
                  _mod_spec(sc, tm), _mod_spec(sh, tm)],
        out_specs=pl.BlockSpec((tm, d), lambda i: (i, 0)),
        out_shape=jax.ShapeDtypeStruct((m, d), out_dtype),
        compiler_params=pltpu.CompilerParams(dimension_semantics=("parallel",),
                                             vmem_limit_bytes=VMEM_LIMIT_BYTES),
        name="prenorm",
    )(x, g8, sc, sh)


def _postmix_kernel(x_ref, o_ref_in, gate_ref, pg_ref, ng_ref, sc_ref, sh_ref, x1_ref, h2_ref):
    x1 = x_ref[...] + gate_ref[...] * _rms(o_ref_in[...], pg_ref[0:1, :])
    x1_ref[...] = x1
    h2_ref[...] = _rms(x1, ng_ref[0:1, :]) * (1.0 + sc_ref[...]) + sh_ref[...]


def postmix(x, o, gate, post_g, next_g, sc, sh):
    m, d = x.shape
    tm = _pick_tile(m, (256, 128, 64, 32, 16, 8))
    b8 = lambda v: jnp.broadcast_to(v.astype(jnp.float32)[None, :], (8, d))
    row = pl.BlockSpec((tm, d), lambda i: (i, 0))
    c8 = pl.BlockSpec((8, d), lambda i: (0, 0))
    return pl.pallas_call(
        _postmix_kernel,
        grid=(m // tm,),
        in_specs=[row, row, _mod_spec(gate, tm), c8, c8, _mod_spec(sc, tm), _mod_spec(sh, tm)],
        out_specs=[row, row],
        out_shape=[jax.ShapeDtypeStruct((m, d), jnp.float32), jax.ShapeDtypeStruct((m, d), jnp.float32)],
        compiler_params=pltpu.CompilerParams(dimension_semantics=("parallel",),
                                             vmem_limit_bytes=VMEM_LIMIT_BYTES),
        name="postmix",
    )(x, o, gate, b8(post_g), b8(next_g), sc, sh)


def _sb_tile(z, carry, v, u_mat, mask):
    sp = jnp.maximum(z, 0.0) + jnp.log(1.0 + jnp.exp(-jnp.abs(z)))
    l1m = -sp
    if mask is not None:
        l1m = jnp.where(mask, l1m, 0.0)
    tk = z.shape[1]
    parts = []
    for c in reversed(range(tk // CUM_W)):
        blk = l1m[:, c * CUM_W:(c + 1) * CUM_W]
        hi, lo = _split_bf16(blk)
        after = (jnp.dot(hi, u_mat, preferred_element_type=jnp.float32)
                 + jnp.dot(lo, u_mat, preferred_element_type=jnp.float32)) + carry
        parts.append(after)
        carry = carry + jnp.sum(blk, axis=-1, keepdims=True)
    after = jnp.concatenate(parts[::-1], axis=1) if len(parts) > 1 else parts[0]
    w = jnp.exp(z - sp + after)
    if mask is not None:
        w = jnp.where(mask, w, 0.0)
    out = jnp.dot(w.astype(jnp.bfloat16), v, preferred_element_type=jnp.float32)
    return out, carry


def _cum_matrix():
    j = lax.broadcasted_iota(jnp.int32, (CUM_W, CUM_W), 0)
    s = lax.broadcasted_iota(jnp.int32, (CUM_W, CUM_W), 1)
    return (j > s).astype(jnp.bfloat16)


def _prompt_attn_kernel(q_ref, k_ref, v_ref, bias_ref, g_ref, u_ref, o_ref, *, tq, tk):
    qi = pl.program_id(1)
    q = q_ref[...]
    bias = bias_ref[0, 0:1, 0:1]
    u_mat = u_ref[...]
    scale = HEAD_DIM ** -0.5
    n_diag = tq // tk

    def logits(j):
        kt = k_ref[pl.ds(pl.multiple_of(j * tk, tk), tk), :]
        z = lax.dot_general(q, kt, (((1,), (1,)), ((), ())), preferred_element_type=jnp.float32)
        return z * scale + bias

    def values(j):
        return v_ref[pl.ds(pl.multiple_of(j * tk, tk), tk), :]

    acc = jnp.zeros((tq, HEAD_DIM), jnp.float32)
    carry = jnp.zeros((tq, 1), jnp.float32)
    row = lax.broadcasted_iota(jnp.int32, (tq, tk), 0)
    col = lax.broadcasted_iota(jnp.int32, (tq, tk), 1)
    for d in reversed(range(n_diag)):
        j = qi * n_diag + d
        mask = (col + d * tk) < row
        out, carry = _sb_tile(logits(j), carry, values(j), u_mat, mask)
        acc = acc + out

    def body(s, state):
        acc, carry = state
        j = qi * n_diag - 1 - s
        out, carry = _sb_tile(logits(j), carry, values(j), u_mat, None)
        return acc + out, carry

    acc, carry = lax.fori_loop(0, qi * n_diag, body, (acc, carry))
    o_ref[...] = _rms(acc, g_ref[0:1, :]).astype(o_ref.dtype)


def prompt_attention(q, k, v, sb_bias, sb_g, *, tq=512, tk=512):
    s, width = q.shape
    n_heads = width // HEAD_DIM
    bias = jnp.broadcast_to(sb_bias.astype(jnp.float32)[:, None, None], (n_heads, 8, 128))
    g = jnp.broadcast_to(sb_g.astype(jnp.float32)[None, :], (8, HEAD_DIM))
    return pl.pallas_call(
        functools.partial(_prompt_attn_kernel, tq=tq, tk=tk),
        grid=(n_heads, s // tq),
        in_specs=[pl.BlockSpec((tq, HEAD_DIM), lambda h, i: (i, h)),
                  pl.BlockSpec((s, HEAD_DIM), lambda h, i: (0, h)),
                  pl.BlockSpec((s, HEAD_DIM), lambda h, i: (0, h)),
                  pl.BlockSpec((1, 8, 128), lambda h, i: (h, 0, 0)),
                  pl.BlockSpec((8, HEAD_DIM), lambda h, i: (0, 0)),
                  pl.BlockSpec((CUM_W, CUM_W), lambda h, i: (0, 0))],
        out_specs=pl.BlockSpec((tq, HEAD_DIM), lambda h, i: (i, h)),
        out_shape=jax.ShapeDtypeStruct((s, width), jnp.bfloat16),
        compiler_params=pltpu.CompilerParams(dimension_semantics=("parallel", "arbitrary"),
                                             vmem_limit_bytes=VMEM_LIMIT_BYTES),
        name="prompt_attention",
    )(q, k, v, bias, g, _cum_matrix())


def _sample_attn_kernel(pt_ref, q_ref, knew_ref, vnew_ref, bias_ref, mask_ref, g_ref, u_ref, *rest,
                        n_pages_step, page, n_heads):
    kp_refs = rest[:n_pages_step]
    vp_refs = rest[n_pages_step:2 * n_pages_step]
    o_ref, kbf, vbf, acc_ref, carry_ref = rest[2 * n_pages_step:]
    c = pl.program_id(1)
    nc = pl.num_programs(1)
    q = q_ref[0]
    bias = bias_ref[...]
    u_mat = u_ref[...]
    scale = HEAD_DIM ** -0.5

    def tile(k_bf, v_bf, carry, mask):
        z = lax.dot_general(q, k_bf, (((1,), (1,)), ((), ())), preferred_element_type=jnp.float32)
        z = z * scale + bias
        return _sb_tile(z, carry, v_bf, u_mat, mask)

    @pl.when(c == 0)
    def _():
        carry0 = jnp.zeros((q.shape[0], 1), jnp.float32)
        out, carry = tile(knew_ref[0].astype(jnp.bfloat16), vnew_ref[0].astype(jnp.bfloat16), carry0,
                          mask_ref[...] > 0.5)
        acc_ref[...] = out
        carry_ref[...] = carry

    @pl.when(c > 0)
    def _():
        for i in range(n_pages_step):
            for h in range(n_heads):
                rows = pl.ds(h, page, stride=n_heads)
                cols = slice(h * HEAD_DIM, (h + 1) * HEAD_DIM)
                kbf[i * page:(i + 1) * page, cols] = kp_refs[i].at[0][rows, :].astype(jnp.bfloat16)
                vbf[i * page:(i + 1) * page, cols] = vp_refs[i].at[0][rows, :].astype(jnp.bfloat16)
        out, carry = tile(kbf[...], vbf[...], carry_ref[...], None)
        acc_ref[...] += out
        carry_ref[...] = carry

    @pl.when(c == nc - 1)
    def _():
        acc = acc_ref[...]
        nq = acc.shape[0] // n_heads
        for h in range(n_heads):
            o = acc[h * nq:(h + 1) * nq, h * HEAD_DIM:(h + 1) * HEAD_DIM]
            o_ref[0, :, h * HEAD_DIM:(h + 1) * HEAD_DIM] = _rms(o, g_ref[0:1, :]).astype(o_ref.dtype)


def sample_attention(q, k_new, v_new, cache_k, cache_v, page_table, sb_bias, sb_g):
    b, ds, width = q.shape
    n_heads = width // HEAD_DIM
    page = cache_k.shape[1] // n_heads
    n_pages = page_table.shape[1]
    pps = PAGES_PER_STEP
    assert n_pages % pps == 0 and (pps * page) % CUM_W == 0
    n_chunks = n_pages // pps
    rows = n_heads * ds
    qh = jnp.transpose(q.reshape(b, ds, n_heads, HEAD_DIM), (0, 2, 1, 3))
    eye = jnp.eye(n_heads, dtype=q.dtype)
    q_bd = (qh[:, :, :, None, :] * eye[None, :, None, :, None]).reshape(b, rows, width)
    pad = CUM_W - ds
    k_pad = jnp.pad(k_new, ((0, 0), (0, pad), (0, 0)))
    v_pad = jnp.pad(v_new, ((0, 0), (0, pad), (0, 0)))
    bias = jnp.repeat(sb_bias.astype(jnp.float32), ds)[:, None]
    qpos = jnp.tile(jnp.arange(ds), n_heads)
    mask = (jnp.arange(CUM_W)[None, :] < qpos[:, None]).astype(jnp.float32)
    g = jnp.broadcast_to(sb_g.astype(jnp.float32)[None, :], (8, HEAD_DIM))

    def page_map(i):
        def f(bi, c, pt):
            cc = jnp.maximum(c - 1, 0)
            return (pt[bi, n_pages - (cc + 1) * pps + i % pps], 0, 0)
        return f

    full = lambda shape: pl.BlockSpec(shape, lambda bi, c, pt: (0,) * len(shape))
    in_specs = [pl.BlockSpec((1, rows, width), lambda bi, c, pt: (bi, 0, 0)),
                pl.BlockSpec((1, CUM_W, width), lambda bi, c, pt: (bi, 0, 0)),
                pl.BlockSpec((1, CUM_W, width), lambda bi, c, pt: (bi, 0, 0)),
                full((rows, 1)), full((rows, CUM_W)), full((8, HEAD_DIM)), full((CUM_W, CUM_W))]
    in_specs += [pl.BlockSpec((1, page * n_heads, HEAD_DIM), page_map(i)) for i in range(2 * pps)]
    grid_spec = pltpu.PrefetchScalarGridSpec(
        num_scalar_prefetch=1, grid=(b, n_chunks + 1), in_specs=in_specs,
        out_specs=pl.BlockSpec((1, ds, width), lambda bi, c, pt: (bi, 0, 0)),
        scratch_shapes=[pltpu.VMEM((pps * page, width), jnp.bfloat16),
                        pltpu.VMEM((pps * page, width), jnp.bfloat16),
                        pltpu.VMEM((rows, width), jnp.float32),
                        pltpu.VMEM((rows, 1), jnp.float32)])
    return pl.pallas_call(
        functools.partial(_sample_attn_kernel, n_pages_step=pps, page=page, n_heads=n_heads),
        grid_spec=grid_spec,
        out_shape=jax.ShapeDtypeStruct((b, ds, width), jnp.bfloat16),
        compiler_params=pltpu.CompilerParams(dimension_semantics=("parallel", "arbitrary"),
                                             vmem_limit_bytes=VMEM_LIMIT_BYTES),
        name="sample_attention",
    )(page_table, q_bd, k_pad, v_pad, bias, mask, g, _cum_matrix(), *([cache_k] * pps), *([cache_v] * pps))


def _router_kernel(h_ref, wr_ref, eb_ref, idx_ref, w_ref):
    h_hi, h_lo = _split_bf16(h_ref[...])
    w_hi, w_lo = _split_bf16(wr_ref[...])
    dot = lambda a, b: jnp.dot(a, b, preferred_element_type=jnp.float32)
    logits = dot(h_hi, w_hi) + (dot(h_lo, w_hi) + dot(h_hi, w_lo))
    s = jax.nn.sigmoid(logits)
    sel = s + eb_ref[0:1, :]
    tm, ne = s.shape
    gsz = ne // N_GROUPS
    lane = lax.broadcasted_iota(jnp.int32, (tm, ne), 1)
    grp = lane // gsz
    neg = -jnp.inf

    def argmax_first(x):
        m = jnp.max(x, axis=-1, keepdims=True)
        i = jnp.min(jnp.where(x == m, lane, ne), axis=-1, keepdims=True)
        return m, i

    glane = lax.broadcasted_iota(jnp.int32, (tm, N_GROUPS), 1)
    gscore = jnp.zeros((tm, N_GROUPS), jnp.float32)
    for g in range(N_GROUPS):
        vals = jnp.where(grp == g, sel, neg)
        m1, i1 = argmax_first(vals)
        m2 = jnp.max(jnp.where(lane == i1, neg, vals), axis=-1, keepdims=True)
        gscore = jnp.where(glane == g, m1 + m2, gscore)
    emask = jnp.zeros((tm, ne), jnp.bool_)
    cur = gscore
    for _ in range(TOPK_GROUPS):
        m = jnp.max(cur, axis=-1, keepdims=True)
        gi = jnp.min(jnp.where(cur == m, glane, N_GROUPS), axis=-1, keepdims=True)
        emask = jnp.logical_or(emask, grp == gi)
        cur = jnp.where(glane == gi, neg, cur)
    olane = lax.broadcasted_iota(jnp.int32, idx_ref.shape, 1)
    idx_out = jnp.zeros(idx_ref.shape, jnp.int32)
    w_out = jnp.zeros(w_ref.shape, jnp.float32)
    cur = jnp.where(emask, sel, neg)
    for k in range(TOP_K):
        _, ei = argmax_first(cur)
        hit = lane == ei
        wk = jnp.sum(jnp.where(hit, s, 0.0), axis=-1, keepdims=True)
        idx_out = jnp.where(olane == k, ei, idx_out)
        w_out = jnp.where(olane == k, wk, w_out)
        cur = jnp.where(hit, neg, cur)
    tot = jnp.sum(w_out, axis=-1, keepdims=True)
    idx_ref[...] = idx_out
    w_ref[...] = w_out / tot * ROUTED_SCALE


def route(h, w_router, e_bias, *, tm=256):
    t, d = h.shape
    assert t % tm == 0
    ne = w_router.shape[1]
    eb = jnp.broadcast_to(e_bias.astype(jnp.float32)[None, :], (8, ne))
    idx, w = pl.pallas_call(
        _router_kernel,
        grid=(t // tm,),
        in_specs=[pl.BlockSpec((tm, d), lambda i: (i, 0)),
                  pl.BlockSpec((d, ne), lambda i: (0, 0)),
                  pl.BlockSpec((8, ne), lambda i: (0, 0))],
        out_specs=[pl.BlockSpec((tm, 128), lambda i: (i, 0)),
                   pl.BlockSpec((tm, 128), lambda i: (i, 0))],
        out_shape=[jax.ShapeDtypeStruct((t, 128), jnp.int32),
                   jax.ShapeDtypeStruct((t, 128), jnp.float32)],
        compiler_params=pltpu.CompilerParams(dimension_semantics=("parallel",),
                                             vmem_limit_bytes=VMEM_LIMIT_BYTES),
        name="moe_router",
    )(h, w_router, eb)
    return idx[:, :TOP_K], w[:, :TOP_K]


def _row_copy(src_hbm, dst, sem, r, i):
    return pltpu.make_async_copy(src_hbm.at[pl.ds(r, 1)], dst.at[pl.ds(i, 1)], sem)


def _wait_rows(src_hbm, dst, sem):
    pltpu.make_async_copy(src_hbm.at[pl.ds(0, dst.shape[0])], dst, sem).wait()


def _start_run(src_hbm, idx_ref, idx_off, dst, dst_off, cnt, sem):
    def pair(p, _):
        for q in range(2):
            i = 2 * p + q
            _row_copy(src_hbm, dst, sem, idx_ref[0, 0, idx_off + i], dst_off + i).start(priority=q)
        return 0
    lax.fori_loop(0, cnt // 2, pair, 0)

    @pl.when(lax.rem(cnt, 2) == 1)
    def _():
        i = cnt - 1
        _row_copy(src_hbm, dst, sem, idx_ref[0, 0, idx_off + i], dst_off + i).start()


def _start_window_rows(src_hbm, lo_ref, hi_ref, off, dst, sem, n):
    _start_run(src_hbm, lo_ref, off, dst, 0, n - off, sem)
    _start_run(src_hbm, hi_ref, 0, dst, n - off, off, sem)


def _start_rows(src_hbm, idx_ref, dst, sem, n):
    _start_run(src_hbm, idx_ref, 0, dst, 0, n, sem)


def _expert_kernel(nused_ref, blk_e_ref, blk_src_ref, h_hbm, lo_ref, hi_ref, nlo_ref, nhi_ref,
                   wg_ref, wu_ref, wd_ref, y_ref, xbuf, sems, *, tm):
    i = pl.program_id(0)
    n_used = nused_ref[0]
    slot = lax.rem(i, 2)
    bf = jnp.bfloat16

    def ffn():
        _wait_rows(h_hbm, xbuf.at[slot], sems.at[slot])
        x = xbuf[slot].astype(bf)
        g = jnp.dot(x, wg_ref[0].astype(bf), preferred_element_type=jnp.float32)
        u = jnp.dot(x, wu_ref[0].astype(bf), preferred_element_type=jnp.float32)
        a = (g * jax.nn.sigmoid(g) * u).astype(bf)
        y_ref[...] = jnp.dot(a, wd_ref[0].astype(bf), preferred_element_type=jnp.float32)

    @pl.when(i == 0)
    def _():
        _start_window_rows(h_hbm, lo_ref, hi_ref, lax.rem(blk_src_ref[0], tm), xbuf.at[0], sems.at[0], tm)

    @pl.when(i + 1 < n_used)
    def _():
        _start_window_rows(h_hbm, nlo_ref, nhi_ref, lax.rem(blk_src_ref[i + 1], tm),
                           xbuf.at[1 - slot], sems.at[1 - slot], tm)
        ffn()

    @pl.when(i + 1 == n_used)
    def _():
        ffn()

    @pl.when(i >= n_used)
    def _():
        y_ref[...] = jnp.zeros_like(y_ref)


def expert_ffn(h, tok_sorted, blk_e, blk_src, n_used, w_gate, w_up, w_down, *, tm=MOE_TM):
    t, d = h.shape
    _, _, f = w_gate.shape
    n_blocks = blk_e.shape[0]
    n_win = tok_sorted.shape[0] // tm
    win = tok_sorted.reshape(n_win, 1, tm)

    def wmap(step_off, win_off):
        def f_(i, nu, be, bs):
            b = jnp.minimum(i + step_off, n_blocks - 1)
            return (bs[b] // tm + win_off, 0, 0)
        return f_

    smem = lambda m: pl.BlockSpec((1, 1, tm), m, memory_space=pltpu.SMEM)
    wspec = lambda shape: pl.BlockSpec(shape, lambda i, nu, be, bs: (be[i], 0, 0))
    grid_spec = pltpu.PrefetchScalarGridSpec(
        num_scalar_prefetch=3, grid=(n_blocks,),
        in_specs=[pl.BlockSpec(memory_space=pl.ANY),
                  smem(wmap(0, 0)), smem(wmap(0, 1)), smem(wmap(1, 0)), smem(wmap(1, 1)),
                  wspec((1, d, f)), wspec((1, d, f)), wspec((1, f, d))],
        out_specs=pl.BlockSpec((tm, d), lambda i, nu, be, bs: (i, 0)),
        scratch_shapes=[pltpu.VMEM((2, tm, d), jnp.float32), pltpu.SemaphoreType.DMA((2,))])
    return pl.pallas_call(
        functools.partial(_expert_kernel, tm=tm),
        grid_spec=grid_spec,
        out_shape=jax.ShapeDtypeStruct((n_blocks * tm, d), jnp.float32),
        compiler_params=pltpu.CompilerParams(dimension_semantics=("arbitrary",),
                                             vmem_limit_bytes=VMEM_LIMIT_BYTES),
        name="moe_experts",
    )(n_used, blk_e, blk_src, h, win, win, win, win, w_gate, w_up, w_down)


def _combine_kernel(y_hbm, pos_ref, pos_next_ref, w_ref, sh_ref, x_ref, gate_ref, g_ref, o_ref,
                    ybuf, sems, *, tb):
    i = pl.program_id(0)
    n = pl.num_programs(0)
    slot = lax.rem(i, 2)
    rows = tb * TOP_K

    def combine():
        _wait_rows(y_hbm, ybuf.at[slot], sems.at[slot])
        w = w_ref[...]
        f = sh_ref[...]
        for k in range(TOP_K):
            f = f + w[:, k:k + 1] * ybuf[slot, k * tb:(k + 1) * tb, :]
        o_ref[...] = x_ref[...] + gate_ref[...] * _rms(f, g_ref[0:1, :])

    @pl.when(i == 0)
    def _():
        _start_rows(y_hbm, pos_ref, ybuf.at[0], sems.at[0], rows)

    @pl.when(i + 1 < n)
    def _():
        _start_rows(y_hbm, pos_next_ref, ybuf.at[1 - slot], sems.at[1 - slot], rows)
        combine()

    @pl.when(i + 1 == n)
    def _():
        combine()


def moe_combine(y_sorted, pos, w, shared, x, gate, post_g, *, tb=COMBINE_TB):
    t, d = x.shape
    assert t % tb == 0
    nb = t // tb
    pos3 = jnp.transpose(pos.reshape(nb, tb, TOP_K), (0, 2, 1)).reshape(nb, 1, tb * TOP_K)
    g = jnp.broadcast_to(post_g.astype(jnp.float32)[None, :], (8, d))
    row = lambda i: (i, 0)
    return pl.pallas_call(
        functools.partial(_combine_kernel, tb=tb),
        grid=(nb,),
        in_specs=[pl.BlockSpec(memory_space=pl.ANY),
                  pl.BlockSpec((1, 1, tb * TOP_K), lambda i: (i, 0, 0), memory_space=pltpu.SMEM),
                  pl.BlockSpec((1, 1, tb * TOP_K), lambda i: (jnp.minimum(i + 1, nb - 1), 0, 0),
                               memory_space=pltpu.SMEM),
                  pl.BlockSpec((tb, TOP_K), row),
                  pl.BlockSpec((tb, d), row), pl.BlockSpec((tb, d), row), _mod_spec(gate, tb),
                  pl.BlockSpec((8, d), lambda i: (0, 0))],
        out_specs=pl.BlockSpec((tb, d), row),
        out_shape=jax.ShapeDtypeStruct((t, d), jnp.float32),
        scratch_shapes=[pltpu.VMEM((2, tb * TOP_K, d), jnp.float32), pltpu.SemaphoreType.DMA((2,))],
        compiler_params=pltpu.CompilerParams(dimension_semantics=("arbitrary",),
                                             vmem_limit_bytes=VMEM_LIMIT_BYTES),
        name="moe_combine",
    )(y_sorted, pos3, pos3, w, shared, x, gate, g)


def dispatch_plan(eidx, *, tm=MOE_TM):
    t = eidx.shape[0]
    n = t * TOP_K
    assert n % tm == 0
    i32 = jnp.int32
    flat_e = eidx.reshape(n).astype(i32)
    a = jnp.arange(n, dtype=i32)
    experts = jnp.arange(N_EXPERTS, dtype=i32)
    se, order = lax.sort((flat_e, a), num_keys=1, is_stable=True)
    counts = jnp.sum((flat_e[:, None] == experts[None, :]).astype(i32), axis=0)
    pcounts = (counts + tm - 1) // tm * tm
    start = jnp.cumsum(counts) - counts
    pend = jnp.cumsum(pcounts)
    pstart = pend - pcounts
    pick = lambda sel, table: jnp.sum(jnp.where(sel[:, None] == experts[None, :], table[None, :], 0), axis=1)
    dest = a + pick(se, pstart - start)
    _, pos = lax.sort((order, dest), num_keys=1)
    n_blocks = -(-(n + N_EXPERTS * (tm - 1)) // tm)
    blk_row = jnp.arange(n_blocks, dtype=i32) * tm
    blk_e = jnp.minimum(jnp.sum((pend[None, :] <= blk_row[:, None]).astype(i32), axis=1), N_EXPERTS - 1)
    blk_src = jnp.clip(pick(blk_e, start - pstart) + blk_row, 0, n)
    tok_sorted = jnp.pad(order // TOP_K, (0, 2 * tm))
    n_used = (pend[-1] // tm).astype(i32).reshape(1)
    return tok_sorted, pos.reshape(t, TOP_K), blk_e, blk_src, n_used


SOLVE_BASE = 8
HALO = 8


def _bmm(a, b):
    return jnp.einsum('bij,bjk->bik', a, b, preferred_element_type=jnp.float32)


def _bmm_nt(a, b):
    return jnp.einsum('bid,bjd->bij', a, b, preferred_element_type=jnp.float32)


def _bmm_hp(a, b):
    a_hi, a_lo = _split_bf16(a)
    b_hi, b_lo = _split_bf16(b)
    return _bmm(a_hi, b_hi) + (_bmm(a_hi, b_lo) + _bmm(a_lo, b_hi))


def _unit_lower_inverse(a, ri, ci):
    c = a.shape[-1]
    s = SOLVE_BASE
    n1 = jnp.where((ri // s) == (ci // s), -a, 0.0)
    p = jnp.where(ri == ci, 1.0, 0.0) + n1
    n2 = _bmm_hp(n1, n1)
    p = p + _bmm_hp(p, n2)
    p = p + _bmm_hp(p, _bmm_hp(n2, n2))
    while s < c:
        join = jnp.where(((ri // (2 * s)) == (ci // (2 * s))) & ((ri // s) != (ci // s)), a, 0.0)
        p = p - _bmm_hp(_bmm_hp(p, join), p)
        s *= 2
    return p


def _gdn_prep_kernel(x_ref, prev_ref, buf_ref, ab_ref, cw_ref, alog_ref, dtb_ref,
                     q_ref, k_ref, v_ref, beta_ref, g_ref, *, t):
    i = pl.program_id(1)
    x = x_ref[0]
    halo = jnp.where(i == 0, buf_ref[0], prev_ref[0])
    full = jnp.concatenate([halo, x], axis=0)
    acc = None
    for j in range(CONV_W):
        off = HALO - (CONV_W - 1) + j
        term = full[off:off + t, :] * cw_ref[j:j + 1, :]
        acc = term if acc is None else acc + term
    c = acc * jax.nn.sigmoid(acc)
    ab = ab_ref[0]
    for h in range(GDN_HEADS):
        lo, hi = h * HEAD_DIM, (h + 1) * HEAD_DIM
        qh = c[:, lo:hi]
        kh = c[:, GDN_WIDTH + lo:GDN_WIDTH + hi]
        q_ref[0, :, lo:hi] = qh * lax.rsqrt(jnp.sum(qh * qh, axis=-1, keepdims=True) + 1e-6) * (HEAD_DIM ** -0.5)
        k_ref[0, :, lo:hi] = kh * lax.rsqrt(jnp.sum(kh * kh, axis=-1, keepdims=True) + 1e-6)
        v_ref[0, :, lo:hi] = c[:, 2 * GDN_WIDTH + lo:2 * GDN_WIDTH + hi]
        beta = jax.nn.sigmoid(ab[:, h:h + 1])
        z = ab[:, GDN_HEADS + h:GDN_HEADS + h + 1] + dtb_ref[0:1, h:h + 1]
        sp = jnp.maximum(z, 0.0) + jnp.log1p(jnp.exp(-jnp.abs(z)))
        g = -jnp.exp(alog_ref[0:1, h:h + 1]) * sp
        beta_ref[0, :, lo:hi] = jnp.broadcast_to(beta, (t, HEAD_DIM))
        g_ref[0, :, lo:hi] = jnp.broadcast_to(g, (t, HEAD_DIM))


def gdn_prep(qkv, ab, conv_buf, conv_w, a_log, dt_bias):
    b, l, w3 = qkv.shape
    t = min(l, 256)
    assert l % t == 0 and t % HALO == 0
    buf8 = jnp.pad(conv_buf, ((0, 0), (HALO - (CONV_W - 1), 0), (0, 0)))
    cw8 = jnp.pad(conv_w, ((0, 8 - CONV_W), (0, 0)))
    row8 = lambda vec: jnp.broadcast_to(jnp.pad(vec.astype(jnp.float32), (0, 128 - vec.shape[0]))[None, :], (8, 128))
    tpb = t // HALO
    out = jax.ShapeDtypeStruct((b, l, GDN_WIDTH), jnp.float32)
    ospec = pl.BlockSpec((1, t, GDN_WIDTH), lambda s, i: (s, i, 0))
    return pl.pallas_call(
        functools.partial(_gdn_prep_kernel, t=t),
        grid=(b, l // t),
        in_specs=[pl.BlockSpec((1, t, w3), lambda s, i: (s, i, 0)),
                  pl.BlockSpec((1, HALO, w3), lambda s, i: (s, jnp.maximum(i * tpb - 1, 0), 0)),
                  pl.BlockSpec((1, HALO, w3), lambda s, i: (s, 0, 0)),
                  pl.BlockSpec((1, t, 2 * GDN_HEADS), lambda s, i: (s, i, 0)),
                  pl.BlockSpec((8, w3), lambda s, i: (0, 0)),
                  pl.BlockSpec((8, 128), lambda s, i: (0, 0)),
                  pl.BlockSpec((8, 128), lambda s, i: (0, 0))],
        out_specs=[ospec] * 5,
        out_shape=[out] * 5,
        compiler_params=pltpu.CompilerParams(dimension_semantics=("parallel", "parallel"),
                                             vmem_limit_bytes=VMEM_LIMIT_BYTES),
        name="gdn_prep",
    )(qkv, qkv, buf8, ab, cw8, row8(a_log), row8(dt_bias))


def _gdn_chunk_kernel(q_ref, k_ref, v_ref, beta_ref, g_ref, w_ref, uv_ref, att_ref, qd_ref, kd_ref, dec_ref,
                      *, n_sub):
    c = GDN_CHUNK
    n_heads = q_ref.shape[1] // HEAD_DIM
    ri = lax.broadcasted_iota(jnp.int32, (c, c), 0)
    ci = lax.broadcasted_iota(jnp.int32, (c, c), 1)
    bf = jnp.bfloat16

    def batched(ref):
        x = ref[...]
        return jnp.concatenate([x[:, h * HEAD_DIM:(h + 1) * HEAD_DIM].reshape(n_sub, c, HEAD_DIM)
                                for h in range(n_heads)], axis=0)

    q, k, v, beta, g = (batched(r) for r in (q_ref, k_ref, v_ref, beta_ref, g_ref))
    nb = n_heads * n_sub
    gc_row = jnp.sum(jnp.where(ri <= ci, g[:, :, :c], 0.0), axis=1, keepdims=True)
    gc_col = jnp.sum(jnp.where(ri == ci, jnp.broadcast_to(gc_row, (nb, c, c)), 0.0), axis=2, keepdims=True)
    diff = gc_col - gc_row
    e_incl = jnp.exp(jnp.where(ri >= ci, diff, -jnp.inf))
    e_strict = jnp.where(ri > ci, e_incl, 0.0)
    kb = k * beta
    kbf = k.astype(bf)
    a_mat = _bmm_nt(kb.astype(bf), kbf) * e_strict
    rhs = jnp.concatenate([kb * jnp.exp(gc_col), v * beta], axis=2)
    x = _bmm_hp(_unit_lower_inverse(a_mat, ri, ci), rhs)
    att = _bmm_nt(q.astype(bf), kbf) * e_incl
    gc_last = gc_col[:, c - 1:c, :]
    qd = q * jnp.exp(gc_col)
    kd = k * jnp.exp(gc_last - gc_col)
    dec = jnp.broadcast_to(jnp.exp(gc_last), (nb, 1, HEAD_DIM))
    att_ref[...] = att.reshape(n_heads, n_sub * c, c).astype(att_ref.dtype)
    for h in range(n_heads):
        cols = slice(h * HEAD_DIM, (h + 1) * HEAD_DIM)
        part = lambda a: a[h * n_sub:(h + 1) * n_sub].reshape(n_sub * a.shape[1], a.shape[2])
        w_ref[:, cols] = part(x[:, :, :HEAD_DIM]).astype(w_ref.dtype)
        uv_ref[:, cols] = part(x[:, :, HEAD_DIM:])
        qd_ref[:, cols] = part(qd).astype(qd_ref.dtype)
        kd_ref[:, cols] = part(kd).astype(kd_ref.dtype)
        dec_ref[:, h:h + 1, :] = dec[h * n_sub:(h + 1) * n_sub]


def gdn_chunk(q, k, v, beta, g, *, n_sub=2):
    r, w = q.shape
    c = GDN_CHUNK
    n_chunks = r // c
    assert r % c == 0 and n_chunks % n_sub == 0
    t = n_sub * c
    n_heads = w // HEAD_DIM
    row = pl.BlockSpec((t, w), lambda i: (i, 0))
    f32, bf = jnp.float32, jnp.bfloat16
    return pl.pallas_call(
        functools.partial(_gdn_chunk_kernel, n_sub=n_sub),
        grid=(n_chunks // n_sub,),
        in_specs=[row] * 5,
        out_specs=[row, row,
                   pl.BlockSpec((n_heads, t, c), lambda i: (0, i, 0)),
                   row, row,
                   pl.BlockSpec((n_sub, n_heads, HEAD_DIM), lambda i: (i, 0, 0))],
        out_shape=[jax.ShapeDtypeStruct((r, w), bf), jax.ShapeDtypeStruct((r, w), f32),
                   jax.ShapeDtypeStruct((n_heads, r, c), bf),
                   jax.ShapeDtypeStruct((r, w), bf), jax.ShapeDtypeStruct((r, w), bf),
                   jax.ShapeDtypeStruct((n_chunks, n_heads, HEAD_DIM), f32)],
        compiler_params=pltpu.CompilerParams(dimension_semantics=("parallel",),
                                             vmem_limit_bytes=VMEM_LIMIT_BYTES),
        name="gdn_chunk",
    )(q, k, v, beta, g)


def _gdn_scan_kernel(w_ref, uv_ref, att_ref, qd_ref, kd_ref, dec_ref, zg_ref, s0_ref, gg_ref,
                     o_ref, sout_ref, state, *, n_sub):
    i = pl.program_id(1)
    c = GDN_CHUNK
    bf = jnp.bfloat16
    n_heads = state.shape[0]

    @pl.when(i == 0)
    def _():
        state[...] = s0_ref[0].astype(jnp.float32)

    def chunk(ci, _):
        rows = pl.ds(pl.multiple_of(ci * c, c), c)
        dec = dec_ref[ci]
        for h in range(n_heads):
            cols = slice(h * HEAD_DIM, (h + 1) * HEAD_DIM)
            s = state[h]
            sb = s.astype(bf)
            u = uv_ref[rows, cols] - jnp.dot(w_ref[rows, cols], sb, preferred_element_type=jnp.float32)
            ub = u.astype(bf)
            o = (jnp.dot(qd_ref[rows, cols], sb, preferred_element_type=jnp.float32)
                 + jnp.dot(att_ref[h, rows, :], ub, preferred_element_type=jnp.float32))
            upd = lax.dot_general(kd_ref[rows, cols], ub, (((0,), (0,)), ((), ())),
                                  preferred_element_type=jnp.float32)
            state[h] = dec[h:h + 1, :] * s + upd
            z = zg_ref[rows, cols]
            o_ref[rows, cols] = (_rms(o, gg_ref[0:1, :]) * (z * jax.nn.sigmoid(z))).astype(o_ref.dtype)
        return 0

    lax.fori_loop(0, n_sub, chunk, 0)

    @pl.when(i == pl.num_programs(1) - 1)
    def _():
        sout_ref[0] = state[...].astype(sout_ref.dtype)


def gdn_scan(w, uv, att, qd, kd, dec, zg, s0, gdn_g):
    r, width = w.shape
    b = s0.shape[0]
    l = r // b
    c = GDN_CHUNK
    n_chunks = l // c
    n_heads = width // HEAD_DIM
    n_sub = 8 if n_chunks % 8 == 0 else n_chunks
    t = n_sub * c
    steps = n_chunks // n_sub
    row = pl.BlockSpec((t, width), lambda s, i: (s * steps + i, 0))
    g8 = jnp.broadcast_to(gdn_g.astype(jnp.float32)[None, :], (8, HEAD_DIM))
    state_spec = pl.BlockSpec((1, n_heads, HEAD_DIM, HEAD_DIM), lambda s, i: (s, 0, 0, 0))
    return pl.pallas_call(
        functools.partial(_gdn_scan_kernel, n_sub=n_sub),
        grid=(b, steps),
        in_specs=[row, row,
                  pl.BlockSpec((n_heads, t, c), lambda s, i: (0, s * steps + i, 0)),
                  row, row,
                  pl.BlockSpec((n_sub, n_heads, HEAD_DIM), lambda s, i: (s * steps + i, 0, 0)),
                  row, state_spec,
                  pl.BlockSpec((8, HEAD_DIM), lambda s, i: (0, 0))],
        out_specs=[row, state_spec],
        out_shape=[jax.ShapeDtypeStruct((r, width), jnp.bfloat16),
                   jax.ShapeDtypeStruct(s0.shape, s0.dtype)],
        scratch_shapes=[pltpu.VMEM((n_heads, HEAD_DIM, HEAD_DIM), jnp.float32)],
        compiler_params=pltpu.CompilerParams(dimension_semantics=("parallel", "arbitrary"),
                                             vmem_limit_bytes=VMEM_LIMIT_BYTES),
        name="gdn_scan",
    )(w, uv, att, qd, kd, dec, zg, s0, g8)


def gdn_mixer(qkv, zg, ab, conv_buf, s0, conv_w, a_log, dt_bias, gdn_g):
    b, l, _ = qkv.shape
    c = GDN_CHUNK
    q, k, v, beta, g = gdn_prep(qkv, ab, conv_buf, conv_w, a_log, dt_bias)
    lp = -(-l // c) * c
    rows = lambda a: jnp.pad(a, ((0, 0), (0, lp - l), (0, 0))).reshape(b * lp, GDN_WIDTH)
    w, uv, att, qd, kd, dec = gdn_chunk(*(rows(a) for a in (q, k, v, beta, g)))
    o, s_new = gdn_scan(w, uv, att, qd, kd, dec, rows(zg), s0, gdn_g)
    keep = CONV_W - 1
    new_buf = qkv[:, l - keep:] if l >= keep else jnp.concatenate([conv_buf, qkv], axis=1)[:, -keep:]
    return o.reshape(b, lp, GDN_WIDTH)[:, :l].reshape(b * l, GDN_WIDTH), new_buf, s_new


def _expand(mod_rows, reps):
    return mod_rows if reps is None else jnp.repeat(mod_rows, reps, axis=0)


def kernel(x_prompt, x_sample, cache_sb_k, cache_sb_v, state_conv, state_gdn, page_table, c_prompt, c_sample,
           w_ada, b_ada, pre_mix_g, post_mix_g, pre_ffn_g, post_ffn_g, w_in, conv_w, a_log, dt_bias, sb_bias,
           sb_out_g, gdn_out_g, w_out, w_router, e_bias, w_gate, w_up, w_down, ws_gate, ws_up, ws_down):
    l = 0
    bp, sp, d = x_prompt.shape
    bs, ss, _ = x_sample.shape
    assert bp == 1
    f32, bf16 = jnp.float32, jnp.bfloat16

    c_all = jnp.concatenate([c_prompt, c_sample], axis=0)
    c_pad = jnp.pad(c_all, ((0, (-c_all.shape[0]) % 8), (0, 0)))
    mod, = project([c_pad], w_ada[l], 0, 6 * d, (f32,), bias=b_ada[l], pre_silu=True)
    mod = mod[:bp + bs].reshape(bp + bs, 6, d)
    mod_p = [mod[:bp, i] for i in range(6)]
    mod_s = [_expand(mod[bp:, i], ss) for i in range(6)]

    groups = (
        dict(x=x_prompt.reshape(sp, d), mod=mod_p, batch=bp, length=sp),
        dict(x=x_sample.reshape(bs * ss, d), mod=mod_s, batch=bs, length=ss),
    )
    n_cache = cache_sb_k.shape[1]
    cache_k = cache_sb_k[l].reshape(n_cache, cache_sb_k.shape[2] * SB_HEADS, HEAD_DIM)
    cache_v = cache_sb_v[l].reshape(n_cache, cache_sb_v.shape[2] * SB_HEADS, HEAD_DIM)
    conv_bufs = (jnp.zeros((bp, CONV_W - 1, 3 * GDN_WIDTH), f32), state_conv[l])
    states = (jnp.zeros((bp, GDN_HEADS, HEAD_DIM, HEAD_DIM), state_gdn.dtype), state_gdn[l])

    results = []
    for gi, grp in enumerate(groups):
        x, (sh1, sc1, g1, sh2, sc2, g2) = grp['x'], grp['mod']
        b, length = grp['batch'], grp['length']
        h = prenorm(x, pre_mix_g[l], sc1, sh1, bf16)
        wi = w_in[l]
        q_sb, = project([h], wi, 0, SB_WIDTH, (bf16,))
        k_sb, k_bf = project([h], wi, SB_WIDTH, SB_WIDTH, (f32, bf16))
        v_sb, v_bf = project([h], wi, 2 * SB_WIDTH, SB_WIDTH, (f32, bf16))
        qkv, = project([h], wi, 3 * SB_WIDTH, 3 * GDN_WIDTH, (f32,))
        zg, = project([h], wi, 3 * SB_WIDTH + 3 * GDN_WIDTH, GDN_WIDTH, (f32,))
        ab, = project([h], wi, 3 * SB_WIDTH + 4 * GDN_WIDTH, 2 * GDN_HEADS, (f32,))
        if gi == 0:
            o_sb = prompt_attention(q_sb, k_bf, v_bf, sb_bias[l], sb_out_g[l])
        else:
            o_sb = sample_attention(q_sb.reshape(b, length, SB_WIDTH), k_sb.reshape(b, length, SB_WIDTH),
                                    v_sb.reshape(b, length, SB_WIDTH), cache_k, cache_v, page_table,
                                    sb_bias[l], sb_out_g[l]).reshape(b * length, SB_WIDTH)
        o_g, new_buf, s_new = gdn_mixer(qkv.reshape(b, length, -1), zg.reshape(b, length, -1),
                                        ab.reshape(b, length, -1), conv_bufs[gi], states[gi],
                                        conv_w[l], a_log[l], dt_bias[l], gdn_out_g[l])
        o, = project([o_sb, o_g], w_out[l], 0, d, (f32,))
        x1, h2 = postmix(x, o, g1, post_mix_g[l], pre_ffn_g[l], sc2, sh2)
        results.append(dict(x1=x1, h2=h2, g2=g2, k=k_sb.reshape(b, length, SB_HEADS, HEAD_DIM),
                            v=v_sb.reshape(b, length, SB_HEADS, HEAD_DIM), conv=new_buf, state=s_new))

    h2 = jnp.concatenate([r['h2'] for r in results], axis=0)
    eidx, ew = route(h2, w_router[l], e_bias[l])
    tok_sorted, pos, blk_e, blk_src, n_used = dispatch_plan(eidx)
    y_sorted = expert_ffn(h2, tok_sorted, blk_e, blk_src, n_used, w_gate[l], w_up[l], w_down[l])
    act = swiglu_up(h2, ws_gate[l], ws_up[l])
    shared, = project([act], ws_down[l], 0, d, (f32,))
    outs, t0 = [], 0
    for r in results:
        t1 = t0 + r['x1'].shape[0]
        outs.append(moe_combine(y_sorted, pos[t0:t1], ew[t0:t1], shared[t0:t1], r['x1'], r['g2'], post_ffn_g[l]))
        t0 = t1

    y_p = outs[0].reshape(bp, sp, d)
    y_s = outs[1].reshape(bs, ss, d)
    st = lambda t: t[None]
    rp, rs = results
    return (y_p, y_s, st(rp['k']), st(rp['v']), st(rp['conv']), st(rp['state']),
            st(rs['k']), st(rs['v']), st(rs['conv']), st(rs['state']))
```

```python
import functools

import jax
import jax.numpy as jnp
from jax import lax
from jax.experimental import pallas as pl
from jax.experimental.pallas import tpu as pltpu

D_MODEL = 2048
HEAD_DIM = 128
SB_HEADS = 8
GDN_HEADS = 8
SB_WIDTH = SB_HEADS * HEAD_DIM
GDN_WIDTH = GDN_HEADS * HEAD_DIM
CONV_W = 4
GDN_CHUNK = 64
N_EXPERTS = 64
TOP_K = 8
N_GROUPS = 8
TOPK_GROUPS = 4
ROUTED_SCALE = 2.5
NORM_EPS = 1e-6

VMEM_LIMIT_BYTES = 48 * 1024 * 1024
CUM_W = 256
PAGES_PER_STEP = 8
MOE_TM = 256
COMBINE_TB = 128
LANE_ROWS = D_MODEL // 128


def _pick_tile(n, candidates):
    for c in candidates:
        if n % c == 0:
            return c
    return n


def _split_bf16(x):
    hi = x.astype(jnp.bfloat16)
    lo = (x - hi.astype(jnp.float32)).astype(jnp.bfloat16)
    return hi, lo


def _rms(x, g_row):
    return x * lax.rsqrt(jnp.mean(x * x, axis=-1, keepdims=True) + NORM_EPS) * g_row


def to_row_major(ref, x):
    m = x.shape[0]
    for c in range(LANE_ROWS):
        ref[pl.ds(c, m, stride=LANE_ROWS), :] = x[:, c * 128:(c + 1) * 128]


def from_row_major(ref, m, row0=0):
    return [ref[pl.ds(row0 * LANE_ROWS + c, m, stride=LANE_ROWS), :] for c in range(LANE_ROWS)]


def _proj_kernel(*refs, n_lhs, k_splits, pre_silu, has_bias, n_out):
    lhs = refs[:n_lhs]
    w_ref = refs[n_lhs]
    pos = n_lhs + 1
    b_ref = refs[pos] if has_bias else None
    pos += int(has_bias)
    outs = refs[pos:pos + n_out]
    wbf = refs[pos + n_out]

    @pl.when(pl.program_id(1) == 0)
    def _():
        wbf[...] = w_ref[...].astype(jnp.bfloat16)

    acc = None
    for x_ref, (k0, k1) in zip(lhs, k_splits):
        x = x_ref[...]
        if pre_silu:
            x = x * jax.nn.sigmoid(x)
        part = jnp.dot(x.astype(jnp.bfloat16), wbf[k0:k1, :], preferred_element_type=jnp.float32)
        acc = part if acc is None else acc + part
    if has_bias:
        acc = acc + b_ref[0:1, :]
    for o in outs:
        o[...] = acc.astype(o.dtype)


def project(lhs, w, col0, ncols, out_dtypes, *, bias=None, pre_silu=False):
    m = lhs[0].shape[0]
    k = w.shape[0]
    k_splits, k0 = [], 0
    for x in lhs:
        k_splits.append((k0, k0 + x.shape[1]))
        k0 += x.shape[1]
    assert k0 == k and m % 8 == 0
    np_ = -(-ncols // 128) * 128
    if np_ != ncols:
        assert bias is None
        w = jnp.pad(lax.slice_in_dim(w, col0, col0 + ncols, axis=1), ((0, 0), (0, np_ - ncols)))
        col0 = 0
    tn = _pick_tile(np_, (1024, 512, 256, 128))
    assert col0 % tn == 0
    tm = _pick_tile(m, (512, 256, 128, 64, 32, 16, 8))
    in_specs = [pl.BlockSpec((tm, x.shape[1]), lambda j, i: (i, 0)) for x in lhs]
    in_specs.append(pl.BlockSpec((k, tn), lambda j, i: (0, col0 // tn + j)))
    args = list(lhs) + [w]
    if bias is not None:
        b = jnp.broadcast_to(bias.astype(jnp.float32)[None, :], (8, bias.shape[0]))
        in_specs.append(pl.BlockSpec((8, tn), lambda j, i: (0, col0 // tn + j)))
        args.append(b)
    outs = pl.pallas_call(
        functools.partial(_proj_kernel, n_lhs=len(lhs), k_splits=tuple(k_splits), pre_silu=pre_silu,
                          has_bias=bias is not None, n_out=len(out_dtypes)),
        grid=(np_ // tn, m // tm),
        in_specs=in_specs,
        out_specs=[pl.BlockSpec((tm, tn), lambda j, i: (i, j)) for _ in out_dtypes],
        out_shape=[jax.ShapeDtypeStruct((m, np_), dt) for dt in out_dtypes],
        scratch_shapes=[pltpu.VMEM((k, tn), jnp.bfloat16)],
        compiler_params=pltpu.CompilerParams(dimension_semantics=("parallel", "arbitrary"),
                                             vmem_limit_bytes=VMEM_LIMIT_BYTES),
        name="project",
    )(*args)
    if np_ != ncols:
        outs = [o[:, :ncols] for o in outs]
    return outs


def _swiglu_up_kernel(x_ref, wg_ref, wu_ref, o_ref, wgb, wub):
    @pl.when(pl.program_id(0) == 0)
    def _():
        wgb[...] = wg_ref[...].astype(jnp.bfloat16)
        wub[...] = wu_ref[...].astype(jnp.bfloat16)
    x = x_ref[...].astype(jnp.bfloat16)
    g = jnp.dot(x, wgb[...], preferred_element_type=jnp.float32)
    u = jnp.dot(x, wub[...], preferred_element_type=jnp.float32)
    o_ref[...] = (g * jax.nn.sigmoid(g) * u).astype(o_ref.dtype)


def swiglu_up(x, wg, wu):
    m, k = x.shape
    f = wg.shape[1]
    tm = _pick_tile(m, (768, 512, 256, 128, 64, 32, 16, 8))
    return pl.pallas_call(
        _swiglu_up_kernel,
        grid=(m // tm,),
        in_specs=[pl.BlockSpec((tm, k), lambda i: (i, 0)),
                  pl.BlockSpec((k, f), lambda i: (0, 0)),
                  pl.BlockSpec((k, f), lambda i: (0, 0))],
        out_specs=pl.BlockSpec((tm, f), lambda i: (i, 0)),
        out_shape=jax.ShapeDtypeStruct((m, f), jnp.bfloat16),
        scratch_shapes=[pltpu.VMEM((k, f), jnp.bfloat16), pltpu.VMEM((k, f), jnp.bfloat16)],
        compiler_params=pltpu.CompilerParams(dimension_semantics=("arbitrary",),
                                             vmem_limit_bytes=VMEM_LIMIT_BYTES),
        name="swiglu_up",
    )(x, wg, wu)


def _mod_spec(a, tm):
    if a.shape[0] == 1:
        return pl.BlockSpec((1, a.shape[1]), lambda i: (0, 0))
    return pl.BlockSpec((tm, a.shape[1]), lambda i: (i, 0))


def _prenorm_kernel(x_ref, g_ref, sc_ref, sh_ref, o_ref):
    y = _rms(x_ref[...], g_ref[0:1, :]) * (1.0 + sc_ref[...]) + sh_ref[...]
    o_ref[...] = y.astype(o_ref.dtype)


def prenorm(x, g, sc, sh, out_dtype):
    m, d = x.shape
    tm = _pick_tile(m, (512, 256, 128, 64, 32, 16, 8))
    g8 = jnp.broadcast_to(g.astype(jnp.float32)[None, :], (8, d))
    return pl.pallas_call(
        _prenorm_kernel,
        grid=(m // tm,),
        in_specs=[pl.BlockSpec((tm, d), lambda i: (i, 0)), pl.BlockSpec((8, d), lambda i: (0, 0)),
                  _mod_spec(sc, tm), _mod_spec(sh, tm)],
        out_specs=pl.BlockSpec((tm, d), lambda i: (i, 0)),
        out_shape=jax.ShapeDtypeStruct((m, d), out_dtype),
        compiler_params=pltpu.CompilerParams(dimension_semantics=("parallel",),
                                             vmem_limit_bytes=VMEM_LIMIT_BYTES),
        name="prenorm",
    )(x, g8, sc, sh)


def _postmix_kernel(x_ref, o_ref_in, gate_ref, pg_ref, ng_ref, sc_ref, sh_ref, x1_ref, h2_ref, h2rm_ref):
    x1 = x_ref[...] + gate_ref[...] * _rms(o_ref_in[...], pg_ref[0:1, :])
    x1_ref[...] = x1
    h2 = _rms(x1, ng_ref[0:1, :]) * (1.0 + sc_ref[...]) + sh_ref[...]
    h2_ref[...] = h2
    to_row_major(h2rm_ref, h2)


def postmix(x, o, gate, post_g, next_g, sc, sh):
    m, d = x.shape
    assert d == LANE_ROWS * 128
    tm = _pick_tile(m, (256, 128, 64, 32, 16, 8))
    b8 = lambda v: jnp.broadcast_to(v.astype(jnp.float32)[None, :], (8, d))
    row = pl.BlockSpec((tm, d), lambda i: (i, 0))
    c8 = pl.BlockSpec((8, d), lambda i: (0, 0))
    return pl.pallas_call(
        _postmix_kernel,
        grid=(m // tm,),
        in_specs=[row, row, _mod_spec(gate, tm), c8, c8, _mod_spec(sc, tm), _mod_spec(sh, tm)],
        out_specs=[row, row, pl.BlockSpec((tm * LANE_ROWS, 128), lambda i: (i, 0))],
        out_shape=[jax.ShapeDtypeStruct((m, d), jnp.float32), jax.ShapeDtypeStruct((m, d), jnp.float32),
                   jax.ShapeDtypeStruct((m * LANE_ROWS, 128), jnp.float32)],
        compiler_params=pltpu.CompilerParams(dimension_semantics=("parallel",),
                                             vmem_limit_bytes=VMEM_LIMIT_BYTES),
        name="postmix",
    )(x, o, gate, b8(post_g), b8(next_g), sc, sh)


def _sb_tile(z, carry, v, u_mat, mask):
    sp = jnp.maximum(z, 0.0) + jnp.log(1.0 + jnp.exp(-jnp.abs(z)))
    l1m = -sp
    if mask is not None:
        l1m = jnp.where(mask, l1m, 0.0)
    tk = z.shape[1]
    parts = []
    for c in reversed(range(tk // CUM_W)):
        blk = l1m[:, c * CUM_W:(c + 1) * CUM_W]
        hi, lo = _split_bf16(blk)
        after = (jnp.dot(hi, u_mat, preferred_element_type=jnp.float32)
                 + jnp.dot(lo, u_mat, preferred_element_type=jnp.float32)) + carry
        parts.append(after)
        carry = carry + jnp.sum(blk, axis=-1, keepdims=True)
    after = jnp.concatenate(parts[::-1], axis=1) if len(parts) > 1 else parts[0]
    w = jnp.exp(z - sp + after)
    if mask is not None:
        w = jnp.where(mask, w, 0.0)
    out = jnp.dot(w.astype(jnp.bfloat16), v, preferred_element_type=jnp.float32)
    return out, carry


def _cum_matrix():
    j = lax.broadcasted_iota(jnp.int32, (CUM_W, CUM_W), 0)
    s = lax.broadcasted_iota(jnp.int32, (CUM_W, CUM_W), 1)
    return (j > s).astype(jnp.bfloat16)


def _prompt_attn_kernel(q_ref, k_ref, v_ref, bias_ref, g_ref, u_ref, o_ref, *, tq, tk):
    qi = pl.program_id(1)
    q = q_ref[...]
    bias = bias_ref[0, 0:1, 0:1]
    u_mat = u_ref[...]
    scale = HEAD_DIM ** -0.5
    n_diag = tq // tk

    def logits(j):
        kt = k_ref[pl.ds(pl.multiple_of(j * tk, tk), tk), :]
        z = lax.dot_general(q, kt, (((1,), (1,)), ((), ())), preferred_element_type=jnp.float32)
        return z * scale + bias

    def values(j):
        return v_ref[pl.ds(pl.multiple_of(j * tk, tk), tk), :]

    acc = jnp.zeros((tq, HEAD_DIM), jnp.float32)
    carry = jnp.zeros((tq, 1), jnp.float32)
    row = lax.broadcasted_iota(jnp.int32, (tq, tk), 0)
    col = lax.broadcasted_iota(jnp.int32, (tq, tk), 1)
    for d in reversed(range(n_diag)):
        j = qi * n_diag + d
        mask = (col + d * tk) < row
        out, carry = _sb_tile(logits(j), carry, values(j), u_mat, mask)
        acc = acc + out

    def body(s, state):
        acc, carry = state
        j = qi * n_diag - 1 - s
        out, carry = _sb_tile(logits(j), carry, values(j), u_mat, None)
        return acc + out, carry

    acc, carry = lax.fori_loop(0, qi * n_diag, body, (acc, carry))
    o_ref[...] = _rms(acc, g_ref[0:1, :]).astype(o_ref.dtype)


def prompt_attention(q, k, v, sb_bias, sb_g, *, tq=512, tk=512):
    s, width = q.shape
    n_heads = width // HEAD_DIM
    bias = jnp.broadcast_to(sb_bias.astype(jnp.float32)[:, None, None], (n_heads, 8, 128))
    g = jnp.broadcast_to(sb_g.astype(jnp.float32)[None, :], (8, HEAD_DIM))
    return pl.pallas_call(
        functools.partial(_prompt_attn_kernel, tq=tq, tk=tk),
        grid=(n_heads, s // tq),
        in_specs=[pl.BlockSpec((tq, HEAD_DIM), lambda h, i: (i, h)),
                  pl.BlockSpec((s, HEAD_DIM), lambda h, i: (0, h)),
                  pl.BlockSpec((s, HEAD_DIM), lambda h, i: (0, h)),
                  pl.BlockSpec((1, 8, 128), lambda h, i: (h, 0, 0)),
                  pl.BlockSpec((8, HEAD_DIM), lambda h, i: (0, 0)),
                  pl.BlockSpec((CUM_W, CUM_W), lambda h, i: (0, 0))],
        out_specs=pl.BlockSpec((tq, HEAD_DIM), lambda h, i: (i, h)),
        out_shape=jax.ShapeDtypeStruct((s, width), jnp.bfloat16),
        compiler_params=pltpu.CompilerParams(dimension_semantics=("parallel", "arbitrary"),
                                             vmem_limit_bytes=VMEM_LIMIT_BYTES),
        name="prompt_attention",
    )(q, k, v, bias, g, _cum_matrix())


def _sample_attn_kernel(pt_ref, q_ref, knew_ref, vnew_ref, bias_ref, mask_ref, g_ref, u_ref, *rest,
                        n_pages_step, page, n_heads):
    kp_refs = rest[:n_pages_step]
    vp_refs = rest[n_pages_step:2 * n_pages_step]
    o_ref, kbf, vbf, acc_ref, carry_ref = rest[2 * n_pages_step:]
    c = pl.program_id(1)
    nc = pl.num_programs(1)
    q = q_ref[0]
    bias = bias_ref[...]
    u_mat = u_ref[...]
    scale = HEAD_DIM ** -0.5

    def tile(k_bf, v_bf, carry, mask):
        z = lax.dot_general(q, k_bf, (((1,), (1,)), ((), ())), preferred_element_type=jnp.float32)
        z = z * scale + bias
        return _sb_tile(z, carry, v_bf, u_mat, mask)

    @pl.when(c == 0)
    def _():
        carry0 = jnp.zeros((q.shape[0], 1), jnp.float32)
        out, carry = tile(knew_ref[0].astype(jnp.bfloat16), vnew_ref[0].astype(jnp.bfloat16), carry0,
                          mask_ref[...] > 0.5)
        acc_ref[...] = out
        carry_ref[...] = carry

    @pl.when(c > 0)
    def _():
        for i in range(n_pages_step):
            for h in range(n_heads):
                rows = pl.ds(h, page, stride=n_heads)
                cols = slice(h * HEAD_DIM, (h + 1) * HEAD_DIM)
                kbf[i * page:(i + 1) * page, cols] = kp_refs[i].at[0][rows, :].astype(jnp.bfloat16)
                vbf[i * page:(i + 1) * page, cols] = vp_refs[i].at[0][rows, :].astype(jnp.bfloat16)
        out, carry = tile(kbf[...], vbf[...], carry_ref[...], None)
        acc_ref[...] += out
        carry_ref[...] = carry

    @pl.when(c == nc - 1)
    def _():
        acc = acc_ref[...]
        nq = acc.shape[0] // n_heads
        for h in range(n_heads):
            o = acc[h * nq:(h + 1) * nq, h * HEAD_DIM:(h + 1) * HEAD_DIM]
            o_ref[0, :, h * HEAD_DIM:(h + 1) * HEAD_DIM] = _rms(o, g_ref[0:1, :]).astype(o_ref.dtype)


def sample_attention(q, k_new, v_new, cache_k, cache_v, page_table, sb_bias, sb_g):
    b, ds, width = q.shape
    n_heads = width // HEAD_DIM
    page = cache_k.shape[1] // n_heads
    n_pages = page_table.shape[1]
    pps = PAGES_PER_STEP
    assert n_pages % pps == 0 and (pps * page) % CUM_W == 0
    n_chunks = n_pages // pps
    rows = n_heads * ds
    qh = jnp.transpose(q.reshape(b, ds, n_heads, HEAD_DIM), (0, 2, 1, 3))
    eye = jnp.eye(n_heads, dtype=q.dtype)
    q_bd = (qh[:, :, :, None, :] * eye[None, :, None, :, None]).reshape(b, rows, width)
    pad = CUM_W - ds
    k_pad = jnp.pad(k_new, ((0, 0), (0, pad), (0, 0)))
    v_pad = jnp.pad(v_new, ((0, 0), (0, pad), (0, 0)))
    bias = jnp.repeat(sb_bias.astype(jnp.float32), ds)[:, None]
    qpos = jnp.tile(jnp.arange(ds), n_heads)
    mask = (jnp.arange(CUM_W)[None, :] < qpos[:, None]).astype(jnp.float32)
    g = jnp.broadcast_to(sb_g.astype(jnp.float32)[None, :], (8, HEAD_DIM))

    def page_map(i):
        def f(bi, c, pt):
            cc = jnp.maximum(c - 1, 0)
            return (pt[bi, n_pages - (cc + 1) * pps + i % pps], 0, 0)
        return f

    full = lambda shape: pl.BlockSpec(shape, lambda bi, c, pt: (0,) * len(shape))
    in_specs = [pl.BlockSpec((1, rows, width), lambda bi, c, pt: (bi, 0, 0)),
                pl.BlockSpec((1, CUM_W, width), lambda bi, c, pt: (bi, 0, 0)),
                pl.BlockSpec((1, CUM_W, width), lambda bi, c, pt: (bi, 0, 0)),
                full((rows, 1)), full((rows, CUM_W)), full((8, HEAD_DIM)), full((CUM_W, CUM_W))]
    in_specs += [pl.BlockSpec((1, page * n_heads, HEAD_DIM), page_map(i)) for i in range(2 * pps)]
    grid_spec = pltpu.PrefetchScalarGridSpec(
        num_scalar_prefetch=1, grid=(b, n_chunks + 1), in_specs=in_specs,
        out_specs=pl.BlockSpec((1, ds, width), lambda bi, c, pt: (bi, 0, 0)),
        scratch_shapes=[pltpu.VMEM((pps * page, width), jnp.bfloat16),
                        pltpu.VMEM((pps * page, width), jnp.bfloat16),
                        pltpu.VMEM((rows, width), jnp.float32),
                        pltpu.VMEM((rows, 1), jnp.float32)])
    return pl.pallas_call(
        functools.partial(_sample_attn_kernel, n_pages_step=pps, page=page, n_heads=n_heads),
        grid_spec=grid_spec,
        out_shape=jax.ShapeDtypeStruct((b, ds, width), jnp.bfloat16),
        compiler_params=pltpu.CompilerParams(dimension_semantics=("parallel", "arbitrary"),
                                             vmem_limit_bytes=VMEM_LIMIT_BYTES),
        name="sample_attention",
    )(page_table, q_bd, k_pad, v_pad, bias, mask, g, _cum_matrix(), *([cache_k] * pps), *([cache_v] * pps))


def _router_kernel(h_ref, wr_ref, eb_ref, idx_ref, w_ref):
    h_hi, h_lo = _split_bf16(h_ref[...])
    w_hi, w_lo = _split_bf16(wr_ref[...])
    dot = lambda a, b: jnp.dot(a, b, preferred_element_type=jnp.float32)
    logits = dot(h_hi, w_hi) + (dot(h_lo, w_hi) + dot(h_hi, w_lo))
    s = jax.nn.sigmoid(logits)
    sel = s + eb_ref[0:1, :]
    tm, ne = s.shape
    gsz = ne // N_GROUPS
    lane = lax.broadcasted_iota(jnp.int32, (tm, ne), 1)
    grp = lane // gsz
    neg = -jnp.inf

    def argmax_first(x):
        m = jnp.max(x, axis=-1, keepdims=True)
        i = jnp.min(jnp.where(x == m, lane, ne), axis=-1, keepdims=True)
        return m, i

    glane = lax.broadcasted_iota(jnp.int32, (tm, N_GROUPS), 1)
    gscore = jnp.zeros((tm, N_GROUPS), jnp.float32)
    for g in range(N_GROUPS):
        vals = jnp.where(grp == g, sel, neg)
        m1, i1 = argmax_first(vals)
        m2 = jnp.max(jnp.where(lane == i1, neg, vals), axis=-1, keepdims=True)
        gscore = jnp.where(glane == g, m1 + m2, gscore)
    emask = jnp.zeros((tm, ne), jnp.bool_)
    cur = gscore
    for _ in range(TOPK_GROUPS):
        m = jnp.max(cur, axis=-1, keepdims=True)
        gi = jnp.min(jnp.where(cur == m, glane, N_GROUPS), axis=-1, keepdims=True)
        emask = jnp.logical_or(emask, grp == gi)
        cur = jnp.where(glane == gi, neg, cur)
    olane = lax.broadcasted_iota(jnp.int32, idx_ref.shape, 1)
    idx_out = jnp.zeros(idx_ref.shape, jnp.int32)
    w_out = jnp.zeros(w_ref.shape, jnp.float32)
    cur = jnp.where(emask, sel, neg)
    for k in range(TOP_K):
        _, ei = argmax_first(cur)
        hit = lane == ei
        wk = jnp.sum(jnp.where(hit, s, 0.0), axis=-1, keepdims=True)
        idx_out = jnp.where(olane == k, ei, idx_out)
        w_out = jnp.where(olane == k, wk, w_out)
        cur = jnp.where(hit, neg, cur)
    tot = jnp.sum(w_out, axis=-1, keepdims=True)
    idx_ref[...] = idx_out
    w_ref[...] = w_out / tot * ROUTED_SCALE


def route(h, w_router, e_bias, *, tm=256):
    t, d = h.shape
    assert t % tm == 0
    ne = w_router.shape[1]
    eb = jnp.broadcast_to(e_bias.astype(jnp.float32)[None, :], (8, ne))
    idx, w = pl.pallas_call(
        _router_kernel,
        grid=(t // tm,),
        in_specs=[pl.BlockSpec((tm, d), lambda i: (i, 0)),
                  pl.BlockSpec((d, ne), lambda i: (0, 0)),
                  pl.BlockSpec((8, ne), lambda i: (0, 0))],
        out_specs=[pl.BlockSpec((tm, 128), lambda i: (i, 0)),
                   pl.BlockSpec((tm, 128), lambda i: (i, 0))],
        out_shape=[jax.ShapeDtypeStruct((t, 128), jnp.int32),
                   jax.ShapeDtypeStruct((t, 128), jnp.float32)],
        compiler_params=pltpu.CompilerParams(dimension_semantics=("parallel",),
                                             vmem_limit_bytes=VMEM_LIMIT_BYTES),
        name="moe_router",
    )(h, w_router, eb)
    return idx[:, :TOP_K], w[:, :TOP_K]


def _row_copy(src_hbm, dst, sem, r, i):
    src = src_hbm.at[pl.ds(pl.multiple_of(r * LANE_ROWS, LANE_ROWS), LANE_ROWS)]
    return pltpu.make_async_copy(src, dst.at[pl.ds(pl.multiple_of(i * LANE_ROWS, LANE_ROWS), LANE_ROWS)], sem)


def _wait_rows(src_hbm, dst, sem):
    pltpu.make_async_copy(src_hbm.at[pl.ds(0, dst.shape[0])], dst, sem).wait()


def _start_run(src_hbm, idx_ref, idx_off, dst, dst_off, cnt, sem):
    def pair(p, _):
        for q in range(2):
            i = 2 * p + q
            _row_copy(src_hbm, dst, sem, idx_ref[0, 0, idx_off + i], dst_off + i).start(priority=q)
        return 0
    lax.fori_loop(0, cnt // 2, pair, 0)

    @pl.when(lax.rem(cnt, 2) == 1)
    def _():
        i = cnt - 1
        _row_copy(src_hbm, dst, sem, idx_ref[0, 0, idx_off + i], dst_off + i).start()


def _start_window_rows(src_hbm, lo_ref, hi_ref, off, dst, sem, n):
    _start_run(src_hbm, lo_ref, off, dst, 0, n - off, sem)
    _start_run(src_hbm, hi_ref, 0, dst, n - off, off, sem)


def _start_rows(src_hbm, idx_ref, dst, sem, n):
    _start_run(src_hbm, idx_ref, 0, dst, 0, n, sem)


def _expert_kernel(nused_ref, blk_e_ref, blk_src_ref, h_hbm, lo_ref, hi_ref, nlo_ref, nhi_ref,
                   wg_ref, wu_ref, wd_ref, y_ref, xbuf, sems, *, tm):
    i = pl.program_id(0)
    n_used = nused_ref[0]
    slot = lax.rem(i, 2)
    bf = jnp.bfloat16

    def ffn():
        _wait_rows(h_hbm, xbuf.at[slot], sems.at[slot])
        x = jnp.concatenate([p.astype(bf) for p in from_row_major(xbuf.at[slot], tm)], axis=1)
        g = jnp.dot(x, wg_ref[0].astype(bf), preferred_element_type=jnp.float32)
        u = jnp.dot(x, wu_ref[0].astype(bf), preferred_element_type=jnp.float32)
        a = (g * jax.nn.sigmoid(g) * u).astype(bf)
        to_row_major(y_ref, jnp.dot(a, wd_ref[0].astype(bf), preferred_element_type=jnp.float32))

    @pl.when(i == 0)
    def _():
        _start_window_rows(h_hbm, lo_ref, hi_ref, lax.rem(blk_src_ref[0], tm), xbuf.at[0], sems.at[0], tm)

    @pl.when(i + 1 < n_used)
    def _():
        _start_window_rows(h_hbm, nlo_ref, nhi_ref, lax.rem(blk_src_ref[i + 1], tm),
                           xbuf.at[1 - slot], sems.at[1 - slot], tm)
        ffn()

    @pl.when(i + 1 == n_used)
    def _():
        ffn()

    @pl.when(i >= n_used)
    def _():
        y_ref[...] = jnp.zeros_like(y_ref)


def expert_ffn(h, tok_sorted, blk_e, blk_src, n_used, w_gate, w_up, w_down, *, tm=MOE_TM):
    _, d, f = w_gate.shape
    assert d == LANE_ROWS * 128
    n_blocks = blk_e.shape[0]
    n_win = tok_sorted.shape[0] // tm
    win = tok_sorted.reshape(n_win, 1, tm)

    def wmap(step_off, win_off):
        def f_(i, nu, be, bs):
            b = jnp.minimum(i + step_off, n_blocks - 1)
            return (bs[b] // tm + win_off, 0, 0)
        return f_

    smem = lambda m: pl.BlockSpec((1, 1, tm), m, memory_space=pltpu.SMEM)
    wspec = lambda shape: pl.BlockSpec(shape, lambda i, nu, be, bs: (be[i], 0, 0))
    grid_spec = pltpu.PrefetchScalarGridSpec(
        num_scalar_prefetch=3, grid=(n_blocks,),
        in_specs=[pl.BlockSpec(memory_space=pl.ANY),
                  smem(wmap(0, 0)), smem(wmap(0, 1)), smem(wmap(1, 0)), smem(wmap(1, 1)),
                  wspec((1, d, f)), wspec((1, d, f)), wspec((1, f, d))],
        out_specs=pl.BlockSpec((tm * LANE_ROWS, 128), lambda i, nu, be, bs: (i, 0)),
        scratch_shapes=[pltpu.VMEM((2, tm * LANE_ROWS, 128), jnp.float32), pltpu.SemaphoreType.DMA((2,))])
    return pl.pallas_call(
        functools.partial(_expert_kernel, tm=tm),
        grid_spec=grid_spec,
        out_shape=jax.ShapeDtypeStruct((n_blocks * tm * LANE_ROWS, 128), jnp.float32),
        compiler_params=pltpu.CompilerParams(dimension_semantics=("arbitrary",),
                                             vmem_limit_bytes=VMEM_LIMIT_BYTES),
        name="moe_experts",
    )(n_used, blk_e, blk_src, h, win, win, win, win, w_gate, w_up, w_down)


def _combine_kernel(y_hbm, pos_ref, pos_next_ref, w_ref, sh_ref, x_ref, gate_ref, g_ref, o_ref,
                    ybuf, sems, *, tb):
    i = pl.program_id(0)
    n = pl.num_programs(0)
    slot = lax.rem(i, 2)
    rows = tb * TOP_K

    def combine():
        _wait_rows(y_hbm, ybuf.at[slot], sems.at[slot])
        w = w_ref[...]
        parts = None
        for k in range(TOP_K):
            yk = from_row_major(ybuf.at[slot], tb, k * tb)
            wk = w[:, k:k + 1]
            parts = [wk * p for p in yk] if parts is None else [a + wk * p for a, p in zip(parts, yk)]
        f = sh_ref[...] + jnp.concatenate(parts, axis=1)
        o_ref[...] = x_ref[...] + gate_ref[...] * _rms(f, g_ref[0:1, :])

    @pl.when(i == 0)
    def _():
        _start_rows(y_hbm, pos_ref, ybuf.at[0], sems.at[0], rows)

    @pl.when(i + 1 < n)
    def _():
        _start_rows(y_hbm, pos_next_ref, ybuf.at[1 - slot], sems.at[1 - slot], rows)
        combine()

    @pl.when(i + 1 == n)
    def _():
        combine()


def moe_combine(y_sorted, pos, w, shared, x, gate, post_g, *, tb=COMBINE_TB):
    t, d = x.shape
    assert d == LANE_ROWS * 128
    assert t % tb == 0
    nb = t // tb
    pos3 = jnp.transpose(pos.reshape(nb, tb, TOP_K), (0, 2, 1)).reshape(nb, 1, tb * TOP_K)
    g = jnp.broadcast_to(post_g.astype(jnp.float32)[None, :], (8, d))
    row = lambda i: (i, 0)
    return pl.pallas_call(
        functools.partial(_combine_kernel, tb=tb),
        grid=(nb,),
        in_specs=[pl.BlockSpec(memory_space=pl.ANY),
                  pl.BlockSpec((1, 1, tb * TOP_K), lambda i: (i, 0, 0), memory_space=pltpu.SMEM),
                  pl.BlockSpec((1, 1, tb * TOP_K), lambda i: (jnp.minimum(i + 1, nb - 1), 0, 0),
                               memory_space=pltpu.SMEM),
                  pl.BlockSpec((tb, TOP_K), row),
                  pl.BlockSpec((tb, d), row), pl.BlockSpec((tb, d), row), _mod_spec(gate, tb),
                  pl.BlockSpec((8, d), lambda i: (0, 0))],
        out_specs=pl.BlockSpec((tb, d), row),
        out_shape=jax.ShapeDtypeStruct((t, d), jnp.float32),
        scratch_shapes=[pltpu.VMEM((2, tb * TOP_K * LANE_ROWS, 128), jnp.float32), pltpu.SemaphoreType.DMA((2,))],
        compiler_params=pltpu.CompilerParams(dimension_semantics=("arbitrary",),
                                             vmem_limit_bytes=VMEM_LIMIT_BYTES),
        name="moe_combine",
    )(y_sorted, pos3, pos3, w, shared, x, gate, g)


def dispatch_plan(eidx, *, tm=MOE_TM):
    t = eidx.shape[0]
    n = t * TOP_K
    assert n % tm == 0
    i32 = jnp.int32
    flat_e = eidx.reshape(n).astype(i32)
    a = jnp.arange(n, dtype=i32)
    experts = jnp.arange(N_EXPERTS, dtype=i32)
    se, order = lax.sort((flat_e, a), num_keys=1, is_stable=True)
    counts = jnp.sum((flat_e[:, None] == experts[None, :]).astype(i32), axis=0)
    pcounts = (counts + tm - 1) // tm * tm
    start = jnp.cumsum(counts) - counts
    pend = jnp.cumsum(pcounts)
    pstart = pend - pcounts
    pick = lambda sel, table: jnp.sum(jnp.where(sel[:, None] == experts[None, :], table[None, :], 0), axis=1)
    dest = a + pick(se, pstart - start)
    _, pos = lax.sort((order, dest), num_keys=1)
    n_blocks = -(-(n + N_EXPERTS * (tm - 1)) // tm)
    blk_row = jnp.arange(n_blocks, dtype=i32) * tm
    blk_e = jnp.minimum(jnp.sum((pend[None, :] <= blk_row[:, None]).astype(i32), axis=1), N_EXPERTS - 1)
    blk_src = jnp.clip(pick(blk_e, start - pstart) + blk_row, 0, n)
    tok_sorted = jnp.pad(order // TOP_K, (0, 2 * tm))
    n_used = (pend[-1] // tm).astype(i32).reshape(1)
    return tok_sorted, pos.reshape(t, TOP_K), blk_e, blk_src, n_used


SOLVE_BASE = 8
HALO = 8


def _bmm(a, b):
    return jnp.einsum('bij,bjk->bik', a, b, preferred_element_type=jnp.float32)


def _bmm_nt(a, b):
    return jnp.einsum('bid,bjd->bij', a, b, preferred_element_type=jnp.float32)


def _bmm_hp(a, b):
    a_hi, a_lo = _split_bf16(a)
    b_hi, b_lo = _split_bf16(b)
    return _bmm(a_hi, b_hi) + (_bmm(a_hi, b_lo) + _bmm(a_lo, b_hi))


def _unit_lower_inverse(a, ri, ci):
    c = a.shape[-1]
    s = SOLVE_BASE
    n1 = jnp.where((ri // s) == (ci // s), -a, 0.0)
    p = jnp.where(ri == ci, 1.0, 0.0) + n1
    n2 = _bmm_hp(n1, n1)
    p = p + _bmm_hp(p, n2)
    p = p + _bmm_hp(p, _bmm_hp(n2, n2))
    while s < c:
        join = jnp.where(((ri // (2 * s)) == (ci // (2 * s))) & ((ri // s) != (ci // s)), a, 0.0)
        p = p - _bmm_hp(_bmm_hp(p, join), p)
        s *= 2
    return p


def _gdn_prep_kernel(x_ref, prev_ref, buf_ref, ab_ref, cw_ref, alog_ref, dtb_ref,
                     q_ref, k_ref, v_ref, beta_ref, g_ref, *, t):
    i = pl.program_id(1)
    x = x_ref[0]
    halo = jnp.where(i == 0, buf_ref[0], prev_ref[0])
    full = jnp.concatenate([halo, x], axis=0)
    acc = None
    for j in range(CONV_W):
        off = HALO - (CONV_W - 1) + j
        term = full[off:off + t, :] * cw_ref[j:j + 1, :]
        acc = term if acc is None else acc + term
    c = acc * jax.nn.sigmoid(acc)
    ab = ab_ref[0]
    for h in range(GDN_HEADS):
        lo, hi = h * HEAD_DIM, (h + 1) * HEAD_DIM
        qh = c[:, lo:hi]
        kh = c[:, GDN_WIDTH + lo:GDN_WIDTH + hi]
        q_ref[0, :, lo:hi] = qh * lax.rsqrt(jnp.sum(qh * qh, axis=-1, keepdims=True) + 1e-6) * (HEAD_DIM ** -0.5)
        k_ref[0, :, lo:hi] = kh * lax.rsqrt(jnp.sum(kh * kh, axis=-1, keepdims=True) + 1e-6)
        v_ref[0, :, lo:hi] = c[:, 2 * GDN_WIDTH + lo:2 * GDN_WIDTH + hi]
        beta = jax.nn.sigmoid(ab[:, h:h + 1])
        z = ab[:, GDN_HEADS + h:GDN_HEADS + h + 1] + dtb_ref[0:1, h:h + 1]
        sp = jnp.maximum(z, 0.0) + jnp.log1p(jnp.exp(-jnp.abs(z)))
        g = -jnp.exp(alog_ref[0:1, h:h + 1]) * sp
        beta_ref[0, :, lo:hi] = jnp.broadcast_to(beta, (t, HEAD_DIM))
        g_ref[0, :, lo:hi] = jnp.broadcast_to(g, (t, HEAD_DIM))


def gdn_prep(qkv, ab, conv_buf, conv_w, a_log, dt_bias):
    b, l, w3 = qkv.shape
    t = min(l, 256)
    assert l % t == 0 and t % HALO == 0
    buf8 = jnp.pad(conv_buf, ((0, 0), (HALO - (CONV_W - 1), 0), (0, 0)))
    cw8 = jnp.pad(conv_w, ((0, 8 - CONV_W), (0, 0)))
    row8 = lambda vec: jnp.broadcast_to(jnp.pad(vec.astype(jnp.float32), (0, 128 - vec.shape[0]))[None, :], (8, 128))
    tpb = t // HALO
    out = jax.ShapeDtypeStruct((b, l, GDN_WIDTH), jnp.float32)
    ospec = pl.BlockSpec((1, t, GDN_WIDTH), lambda s, i: (s, i, 0))
    return pl.pallas_call(
        functools.partial(_gdn_prep_kernel, t=t),
        grid=(b, l // t),
        in_specs=[pl.BlockSpec((1, t, w3), lambda s, i: (s, i, 0)),
                  pl.BlockSpec((1, HALO, w3), lambda s, i: (s, jnp.maximum(i * tpb - 1, 0), 0)),
                  pl.BlockSpec((1, HALO, w3), lambda s, i: (s, 0, 0)),
                  pl.BlockSpec((1, t, 2 * GDN_HEADS), lambda s, i: (s, i, 0)),
                  pl.BlockSpec((8, w3), lambda s, i: (0, 0)),
                  pl.BlockSpec((8, 128), lambda s, i: (0, 0)),
                  pl.BlockSpec((8, 128), lambda s, i: (0, 0))],
        out_specs=[ospec] * 5,
        out_shape=[out] * 5,
        compiler_params=pltpu.CompilerParams(dimension_semantics=("parallel", "parallel"),
                                             vmem_limit_bytes=VMEM_LIMIT_BYTES),
        name="gdn_prep",
    )(qkv, qkv, buf8, ab, cw8, row8(a_log), row8(dt_bias))


def _gdn_chunk_kernel(q_ref, k_ref, v_ref, beta_ref, g_ref, w_ref, uv_ref, att_ref, qd_ref, kd_ref, dec_ref,
                      *, n_sub):
    c = GDN_CHUNK
    n_heads = q_ref.shape[1] // HEAD_DIM
    ri = lax.broadcasted_iota(jnp.int32, (c, c), 0)
    ci = lax.broadcasted_iota(jnp.int32, (c, c), 1)
    bf = jnp.bfloat16

    def batched(ref):
        x = ref[...]
        return jnp.concatenate([x[:, h * HEAD_DIM:(h + 1) * HEAD_DIM].reshape(n_sub, c, HEAD_DIM)
                                for h in range(n_heads)], axis=0)

    q, k, v, beta, g = (batched(r) for r in (q_ref, k_ref, v_ref, beta_ref, g_ref))
    nb = n_heads * n_sub
    gc_row = jnp.sum(jnp.where(ri <= ci, g[:, :, :c], 0.0), axis=1, keepdims=True)
    gc_col = jnp.sum(jnp.where(ri == ci, jnp.broadcast_to(gc_row, (nb, c, c)), 0.0), axis=2, keepdims=True)
    diff = gc_col - gc_row
    e_incl = jnp.exp(jnp.where(ri >= ci, diff, -jnp.inf))
    e_strict = jnp.where(ri > ci, e_incl, 0.0)
    kb = k * beta
    kbf = k.astype(bf)
    a_mat = _bmm_nt(kb.astype(bf), kbf) * e_strict
    rhs = jnp.concatenate([kb * jnp.exp(gc_col), v * beta], axis=2)
    x = _bmm_hp(_unit_lower_inverse(a_mat, ri, ci), rhs)
    att = _bmm_nt(q.astype(bf), kbf) * e_incl
    gc_last = gc_col[:, c - 1:c, :]
    qd = q * jnp.exp(gc_col)
    kd = k * jnp.exp(gc_last - gc_col)
    dec = jnp.broadcast_to(jnp.exp(gc_last), (nb, 1, HEAD_DIM))
    att_ref[...] = att.reshape(n_heads, n_sub * c, c).astype(att_ref.dtype)
    for h in range(n_heads):
        cols = slice(h * HEAD_DIM, (h + 1) * HEAD_DIM)
        part = lambda a: a[h * n_sub:(h + 1) * n_sub].reshape(n_sub * a.shape[1], a.shape[2])
        w_ref[:, cols] = part(x[:, :, :HEAD_DIM]).astype(w_ref.dtype)
        uv_ref[:, cols] = part(x[:, :, HEAD_DIM:])
        qd_ref[:, cols] = part(qd).astype(qd_ref.dtype)
        kd_ref[:, cols] = part(kd).astype(kd_ref.dtype)
        dec_ref[:, h:h + 1, :] = dec[h * n_sub:(h + 1) * n_sub]


def gdn_chunk(q, k, v, beta, g, *, n_sub=2):
    r, w = q.shape
    c = GDN_CHUNK
    n_chunks = r // c
    assert r % c == 0 and n_chunks % n_sub == 0
    t = n_sub * c
    n_heads = w // HEAD_DIM
    row = pl.BlockSpec((t, w), lambda i: (i, 0))
    f32, bf = jnp.float32, jnp.bfloat16
    return pl.pallas_call(
        functools.partial(_gdn_chunk_kernel, n_sub=n_sub),
        grid=(n_chunks // n_sub,),
        in_specs=[row] * 5,
        out_specs=[row, row,
                   pl.BlockSpec((n_heads, t, c), lambda i: (0, i, 0)),
                   row, row,
                   pl.BlockSpec((n_sub, n_heads, HEAD_DIM), lambda i: (i, 0, 0))],
        out_shape=[jax.ShapeDtypeStruct((r, w), bf), jax.ShapeDtypeStruct((r, w), f32),
                   jax.ShapeDtypeStruct((n_heads, r, c), bf),
                   jax.ShapeDtypeStruct((r, w), bf), jax.ShapeDtypeStruct((r, w), bf),
                   jax.ShapeDtypeStruct((n_chunks, n_heads, HEAD_DIM), f32)],
        compiler_params=pltpu.CompilerParams(dimension_semantics=("parallel",),
                                             vmem_limit_bytes=VMEM_LIMIT_BYTES),
        name="gdn_chunk",
    )(q, k, v, beta, g)


def _gdn_scan_kernel(w_ref, uv_ref, att_ref, qd_ref, kd_ref, dec_ref, zg_ref, s0_ref, gg_ref,
                     o_ref, sout_ref, state, *, n_sub):
    i = pl.program_id(1)
    c = GDN_CHUNK
    bf = jnp.bfloat16
    n_heads = state.shape[0]

    @pl.when(i == 0)
    def _():
        state[...] = s0_ref[0].astype(jnp.float32)

    def chunk(ci, _):
        rows = pl.ds(pl.multiple_of(ci * c, c), c)
        dec = dec_ref[ci]
        for h in range(n_heads):
            cols = slice(h * HEAD_DIM, (h + 1) * HEAD_DIM)
            s = state[h]
            sb = s.astype(bf)
            u = uv_ref[rows, cols] - jnp.dot(w_ref[rows, cols], sb, preferred_element_type=jnp.float32)
            ub = u.astype(bf)
            o = (jnp.dot(qd_ref[rows, cols], sb, preferred_element_type=jnp.float32)
                 + jnp.dot(att_ref[h, rows, :], ub, preferred_element_type=jnp.float32))
            upd = lax.dot_general(kd_ref[rows, cols], ub, (((0,), (0,)), ((), ())),
                                  preferred_element_type=jnp.float32)
            state[h] = dec[h:h + 1, :] * s + upd
            z = zg_ref[rows, cols]
            o_ref[rows, cols] = (_rms(o, gg_ref[0:1, :]) * (z * jax.nn.sigmoid(z))).astype(o_ref.dtype)
        return 0

    lax.fori_loop(0, n_sub, chunk, 0)

    @pl.when(i == pl.num_programs(1) - 1)
    def _():
        sout_ref[0] = state[...].astype(sout_ref.dtype)


def gdn_scan(w, uv, att, qd, kd, dec, zg, s0, gdn_g):
    r, width = w.shape
    b = s0.shape[0]
    l = r // b
    c = GDN_CHUNK
    n_chunks = l // c
    n_heads = width // HEAD_DIM
    n_sub = 8 if n_chunks % 8 == 0 else n_chunks
    t = n_sub * c
    steps = n_chunks // n_sub
    row = pl.BlockSpec((t, width), lambda s, i: (s * steps + i, 0))
    g8 = jnp.broadcast_to(gdn_g.astype(jnp.float32)[None, :], (8, HEAD_DIM))
    state_spec = pl.BlockSpec((1, n_heads, HEAD_DIM, HEAD_DIM), lambda s, i: (s, 0, 0, 0))
    return pl.pallas_call(
        functools.partial(_gdn_scan_kernel, n_sub=n_sub),
        grid=(b, steps),
        in_specs=[row, row,
                  pl.BlockSpec((n_heads, t, c), lambda s, i: (0, s * steps + i, 0)),
                  row, row,
                  pl.BlockSpec((n_sub, n_heads, HEAD_DIM), lambda s, i: (s * steps + i, 0, 0)),
                  row, state_spec,
                  pl.BlockSpec((8, HEAD_DIM), lambda s, i: (0, 0))],
        out_specs=[row, state_spec],
        out_shape=[jax.ShapeDtypeStruct((r, width), jnp.bfloat16),
                   jax.ShapeDtypeStruct(s0.shape, s0.dtype)],
        scratch_shapes=[pltpu.VMEM((n_heads, HEAD_DIM, HEAD_DIM), jnp.float32)],
        compiler_params=pltpu.CompilerParams(dimension_semantics=("parallel", "arbitrary"),
                                             vmem_limit_bytes=VMEM_LIMIT_BYTES),
        name="gdn_scan",
    )(w, uv, att, qd, kd, dec, zg, s0, g8)


def gdn_mixer(qkv, zg, ab, conv_buf, s0, conv_w, a_log, dt_bias, gdn_g):
    b, l, _ = qkv.shape
    c = GDN_CHUNK
    q, k, v, beta, g = gdn_prep(qkv, ab, conv_buf, conv_w, a_log, dt_bias)
    lp = -(-l // c) * c
    rows = lambda a: jnp.pad(a, ((0, 0), (0, lp - l), (0, 0))).reshape(b * lp, GDN_WIDTH)
    w, uv, att, qd, kd, dec = gdn_chunk(*(rows(a) for a in (q, k, v, beta, g)))
    o, s_new = gdn_scan(w, uv, att, qd, kd, dec, rows(zg), s0, gdn_g)
    keep = CONV_W - 1
    new_buf = qkv[:, l - keep:] if l >= keep else jnp.concatenate([conv_buf, qkv], axis=1)[:, -keep:]
    return o.reshape(b, lp, GDN_WIDTH)[:, :l].reshape(b * l, GDN_WIDTH), new_buf, s_new


def _expand(mod_rows, reps):
    return mod_rows if reps is None else jnp.repeat(mod_rows, reps, axis=0)


def kernel(x_prompt, x_sample, cache_sb_k, cache_sb_v, state_conv, state_gdn, page_table, c_prompt, c_sample,
           w_ada, b_ada, pre_mix_g, post_mix_g, pre_ffn_g, post_ffn_g, w_in, conv_w, a_log, dt_bias, sb_bias,
           sb_out_g, gdn_out_g, w_out, w_router, e_bias, w_gate, w_up, w_down, ws_gate, ws_up, ws_down):
    l = 0
    bp, sp, d = x_prompt.shape
    bs, ss, _ = x_sample.shape
    assert bp == 1
    f32, bf16 = jnp.float32, jnp.bfloat16

    c_all = jnp.concatenate([c_prompt, c_sample], axis=0)
    c_pad = jnp.pad(c_all, ((0, (-c_all.shape[0]) % 8), (0, 0)))
    mod, = project([c_pad], w_ada[l], 0, 6 * d, (f32,), bias=b_ada[l], pre_silu=True)
    mod = mod[:bp + bs].reshape(bp + bs, 6, d)
    mod_p = [mod[:bp, i] for i in range(6)]
    mod_s = [_expand(mod[bp:, i], ss) for i in range(6)]

    groups = (
        dict(x=x_prompt.reshape(sp, d), mod=mod_p, batch=bp, length=sp),
        dict(x=x_sample.reshape(bs * ss, d), mod=mod_s, batch=bs, length=ss),
    )
    n_cache = cache_sb_k.shape[1]
    cache_k = cache_sb_k[l].reshape(n_cache, cache_sb_k.shape[2] * SB_HEADS, HEAD_DIM)
    cache_v = cache_sb_v[l].reshape(n_cache, cache_sb_v.shape[2] * SB_HEADS, HEAD_DIM)
    conv_bufs = (jnp.zeros((bp, CONV_W - 1, 3 * GDN_WIDTH), f32), state_conv[l])
    states = (jnp.zeros((bp, GDN_HEADS, HEAD_DIM, HEAD_DIM), state_gdn.dtype), state_gdn[l])

    results = []
    for gi, grp in enumerate(groups):
        x, (sh1, sc1, g1, sh2, sc2, g2) = grp['x'], grp['mod']
        b, length = grp['batch'], grp['length']
        h = prenorm(x, pre_mix_g[l], sc1, sh1, bf16)
        wi = w_in[l]
        q_sb, = project([h], wi, 0, SB_WIDTH, (bf16,))
        k_sb, k_bf = project([h], wi, SB_WIDTH, SB_WIDTH, (f32, bf16))
        v_sb, v_bf = project([h], wi, 2 * SB_WIDTH, SB_WIDTH, (f32, bf16))
        qkv, = project([h], wi, 3 * SB_WIDTH, 3 * GDN_WIDTH, (f32,))
        zg, = project([h], wi, 3 * SB_WIDTH + 3 * GDN_WIDTH, GDN_WIDTH, (f32,))
        ab, = project([h], wi, 3 * SB_WIDTH + 4 * GDN_WIDTH, 2 * GDN_HEADS, (f32,))
        if gi == 0:
            o_sb = prompt_attention(q_sb, k_bf, v_bf, sb_bias[l], sb_out_g[l])
        else:
            o_sb = sample_attention(q_sb.reshape(b, length, SB_WIDTH), k_sb.reshape(b, length, SB_WIDTH),
                                    v_sb.reshape(b, length, SB_WIDTH), cache_k, cache_v, page_table,
                                    sb_bias[l], sb_out_g[l]).reshape(b * length, SB_WIDTH)
        o_g, new_buf, s_new = gdn_mixer(qkv.reshape(b, length, -1), zg.reshape(b, length, -1),
                                        ab.reshape(b, length, -1), conv_bufs[gi], states[gi],
                                        conv_w[l], a_log[l], dt_bias[l], gdn_out_g[l])
        o, = project([o_sb, o_g], w_out[l], 0, d, (f32,))
        x1, h2, h2rm = postmix(x, o, g1, post_mix_g[l], pre_ffn_g[l], sc2, sh2)
        results.append(dict(x1=x1, h2=h2, h2rm=h2rm, g2=g2, k=k_sb.reshape(b, length, SB_HEADS, HEAD_DIM),
                            v=v_sb.reshape(b, length, SB_HEADS, HEAD_DIM), conv=new_buf, state=s_new))

    h2 = jnp.concatenate([r['h2'] for r in results], axis=0)
    h2rm = jnp.concatenate([r['h2rm'] for r in results], axis=0)
    eidx, ew = route(h2, w_router[l], e_bias[l])
    tok_sorted, pos, blk_e, blk_src, n_used = dispatch_plan(eidx)
    y_sorted = expert_ffn(h2rm, tok_sorted, blk_e, blk_src, n_used, w_gate[l], w_up[l], w_down[l])
    act = swiglu_up(h2, ws_gate[l], ws_up[l])
    shared, = project([act], ws_down[l], 0, d, (f32,))
    outs, t0 = [], 0
    for r in results:
        t1 = t0 + r['x1'].shape[0]
        outs.append(moe_combine(y_sorted, pos[t0:t1], ew[t0:t1], shared[t0:t1], r['x1'], r['g2'], post_ffn_g[l]))
        t0 = t1

    y_p = outs[0].reshape(bp, sp, d)
    y_s = outs[1].reshape(bs, ss, d)
    st = lambda t: t[None]
    rp, rs = results
    return (y_p, y_s, st(rp['k']), st(rp['v']), st(rp['conv']), st(rp['state']),
            st(rs['k']), st(rs['v']), st(rs['conv']), st(rs['state']))
```

```python
import functools

import jax
import jax.numpy as jnp
from jax import lax
from jax.experimental import pallas as pl
from jax.experimental.pallas import tpu as pltpu

D_MODEL = 2048
HEAD_DIM = 128
SB_HEADS = 8
GDN_HEADS = 8
SB_WIDTH = SB_HEADS * HEAD_DIM
GDN_WIDTH = GDN_HEADS * HEAD_DIM
CONV_W = 4
GDN_CHUNK = 64
N_EXPERTS = 64
TOP_K = 8
N_GROUPS = 8
TOPK_GROUPS = 4
ROUTED_SCALE = 2.5
NORM_EPS = 1e-6

VMEM_LIMIT_BYTES = 48 * 1024 * 1024
CUM_W = 256
PAGES_PER_STEP = 8
MOE_TM = 256
COMBINE_TB = 128
LANE_ROWS = D_MODEL // 128


def _pick_tile(n, candidates):
    for c in candidates:
        if n % c == 0:
            return c
    return n


def _split_bf16(x):
    hi = x.astype(jnp.bfloat16)
    lo = (x - hi.astype(jnp.float32)).astype(jnp.bfloat16)
    return hi, lo


def _rms(x, g_row):
    return x * lax.rsqrt(jnp.mean(x * x, axis=-1, keepdims=True) + NORM_EPS) * g_row


def to_row_major(ref, x):
    m = x.shape[0]
    for c in range(LANE_ROWS):
        ref[pl.ds(c, m, stride=LANE_ROWS), :] = x[:, c * 128:(c + 1) * 128]


def from_row_major(ref, m, row0=0):
    return [ref[pl.ds(row0 * LANE_ROWS + c, m, stride=LANE_ROWS), :] for c in range(LANE_ROWS)]


def _proj_kernel(*refs, n_lhs, k_splits, pre_silu, has_bias, n_out):
    lhs = refs[:n_lhs]
    w_ref = refs[n_lhs]
    pos = n_lhs + 1
    b_ref = refs[pos] if has_bias else None
    pos += int(has_bias)
    outs = refs[pos:pos + n_out]
    wbf = refs[pos + n_out]

    @pl.when(pl.program_id(1) == 0)
    def _():
        wbf[...] = w_ref[...].astype(jnp.bfloat16)

    acc = None
    for x_ref, (k0, k1) in zip(lhs, k_splits):
        x = x_ref[...]
        if pre_silu:
            x = x * jax.nn.sigmoid(x)
        part = jnp.dot(x.astype(jnp.bfloat16), wbf[k0:k1, :], preferred_element_type=jnp.float32)
        acc = part if acc is None else acc + part
    if has_bias:
        acc = acc + b_ref[0:1, :]
    for o in outs:
        o[...] = acc.astype(o.dtype)


def project(lhs, w, col0, ncols, out_dtypes, *, bias=None, pre_silu=False):
    m = lhs[0].shape[0]
    k = w.shape[0]
    k_splits, k0 = [], 0
    for x in lhs:
        k_splits.append((k0, k0 + x.shape[1]))
        k0 += x.shape[1]
    assert k0 == k and m % 8 == 0
    np_ = -(-ncols // 128) * 128
    if np_ != ncols:
        assert bias is None
        w = jnp.pad(lax.slice_in_dim(w, col0, col0 + ncols, axis=1), ((0, 0), (0, np_ - ncols)))
        col0 = 0
    tn = _pick_tile(np_, (1024, 512, 256, 128))
    assert col0 % tn == 0
    tm = _pick_tile(m, (512, 256, 128, 64, 32, 16, 8))
    in_specs = [pl.BlockSpec((tm, x.shape[1]), lambda j, i: (i, 0)) for x in lhs]
    in_specs.append(pl.BlockSpec((k, tn), lambda j, i: (0, col0 // tn + j)))
    args = list(lhs) + [w]
    if bias is not None:
        b = jnp.broadcast_to(bias.astype(jnp.float32)[None, :], (8, bias.shape[0]))
        in_specs.append(pl.BlockSpec((8, tn), lambda j, i: (0, col0 // tn + j)))
        args.append(b)
    outs = pl.pallas_call(
        functools.partial(_proj_kernel, n_lhs=len(lhs), k_splits=tuple(k_splits), pre_silu=pre_silu,
                          has_bias=bias is not None, n_out=len(out_dtypes)),
        grid=(np_ // tn, m // tm),
        in_specs=in_specs,
        out_specs=[pl.BlockSpec((tm, tn), lambda j, i: (i, j)) for _ in out_dtypes],
        out_shape=[jax.ShapeDtypeStruct((m, np_), dt) for dt in out_dtypes],
        scratch_shapes=[pltpu.VMEM((k, tn), jnp.bfloat16)],
        compiler_params=pltpu.CompilerParams(dimension_semantics=("parallel", "arbitrary"),
                                             vmem_limit_bytes=VMEM_LIMIT_BYTES),
        name="project",
    )(*args)
    if np_ != ncols:
        outs = [o[:, :ncols] for o in outs]
    return outs


def _swiglu_up_kernel(x_ref, wg_ref, wu_ref, o_ref, wgb, wub):
    @pl.when(pl.program_id(0) == 0)
    def _():
        wgb[...] = wg_ref[...].astype(jnp.bfloat16)
        wub[...] = wu_ref[...].astype(jnp.bfloat16)
    x = x_ref[...].astype(jnp.bfloat16)
    g = jnp.dot(x, wgb[...], preferred_element_type=jnp.float32)
    u = jnp.dot(x, wub[...], preferred_element_type=jnp.float32)
    o_ref[...] = (g * jax.nn.sigmoid(g) * u).astype(o_ref.dtype)


def swiglu_up(x, wg, wu):
    m, k = x.shape
    f = wg.shape[1]
    tm = _pick_tile(m, (768, 512, 256, 128, 64, 32, 16, 8))
    return pl.pallas_call(
        _swiglu_up_kernel,
        grid=(m // tm,),
        in_specs=[pl.BlockSpec((tm, k), lambda i: (i, 0)),
                  pl.BlockSpec((k, f), lambda i: (0, 0)),
                  pl.BlockSpec((k, f), lambda i: (0, 0))],
        out_specs=pl.BlockSpec((tm, f), lambda i: (i, 0)),
        out_shape=jax.ShapeDtypeStruct((m, f), jnp.bfloat16),
        scratch_shapes=[pltpu.VMEM((k, f), jnp.bfloat16), pltpu.VMEM((k, f), jnp.bfloat16)],
        compiler_params=pltpu.CompilerParams(dimension_semantics=("arbitrary",),
                                             vmem_limit_bytes=VMEM_LIMIT_BYTES),
        name="swiglu_up",
    )(x, wg, wu)


def _mod_spec(a, tm):
    if a.shape[0] == 1:
        return pl.BlockSpec((1, a.shape[1]), lambda i: (0, 0))
    return pl.BlockSpec((tm, a.shape[1]), lambda i: (i, 0))


def _prenorm_kernel(x_ref, g_ref, sc_ref, sh_ref, o_ref):
    y = _rms(x_ref[...], g_ref[0:1, :]) * (1.0 + sc_ref[...]) + sh_ref[...]
    o_ref[...] = y.astype(o_ref.dtype)


def prenorm(x, g, sc, sh, out_dtype):
    m, d = x.shape
    tm = _pick_tile(m, (512, 256, 128, 64, 32, 16, 8))
    g8 = jnp.broadcast_to(g.astype(jnp.float32)[None, :], (8, d))
    return pl.pallas_call(
        _prenorm_kernel,
        grid=(m // tm,),
        in_specs=[pl.BlockSpec((tm, d), lambda i: (i, 0)), pl.BlockSpec((8, d), lambda i: (0, 0)),
                  _mod_spec(sc, tm), _mod_spec(sh, tm)],
        out_specs=pl.BlockSpec((tm, d), lambda i: (i, 0)),
        out_shape=jax.ShapeDtypeStruct((m, d), out_dtype),
        compiler_params=pltpu.CompilerParams(dimension_semantics=("parallel",),
                                             vmem_limit_bytes=VMEM_LIMIT_BYTES),
        name="prenorm",
    )(x, g8, sc, sh)


def _postmix_kernel(x_ref, o_ref_in, gate_ref, pg_ref, ng_ref, sc_ref, sh_ref, x1_ref, h2_ref, h2rm_ref):
    x1 = x_ref[...] + gate_ref[...] * _rms(o_ref_in[...], pg_ref[0:1, :])
    x1_ref[...] = x1
    h2 = _rms(x1, ng_ref[0:1, :]) * (1.0 + sc_ref[...]) + sh_ref[...]
    h2_ref[...] = h2
    to_row_major(h2rm_ref, h2)


def postmix(x, o, gate, post_g, next_g, sc, sh):
    m, d = x.shape
    assert d == LANE_ROWS * 128
    tm = _pick_tile(m, (256, 128, 64, 32, 16, 8))
    b8 = lambda v: jnp.broadcast_to(v.astype(jnp.float32)[None, :], (8, d))
    row = pl.BlockSpec((tm, d), lambda i: (i, 0))
    c8 = pl.BlockSpec((8, d), lambda i: (0, 0))
    return pl.pallas_call(
        _postmix_kernel,
        grid=(m // tm,),
        in_specs=[row, row, _mod_spec(gate, tm), c8, c8, _mod_spec(sc, tm), _mod_spec(sh, tm)],
        out_specs=[row, row, pl.BlockSpec((tm * LANE_ROWS, 128), lambda i: (i, 0))],
        out_shape=[jax.ShapeDtypeStruct((m, d), jnp.float32), jax.ShapeDtypeStruct((m, d), jnp.float32),
                   jax.ShapeDtypeStruct((m * LANE_ROWS, 128), jnp.float32)],
        compiler_params=pltpu.CompilerParams(dimension_semantics=("parallel",),
                                             vmem_limit_bytes=VMEM_LIMIT_BYTES),
        name="postmix",
    )(x, o, gate, b8(post_g), b8(next_g), sc, sh)


def _sb_tile(z, carry, v, u_mat, mask):
    sp = jnp.maximum(z, 0.0) + jnp.log(1.0 + jnp.exp(-jnp.abs(z)))
    l1m = -sp
    if mask is not None:
        l1m = jnp.where(mask, l1m, 0.0)
    tk = z.shape[1]
    parts = []
    for c in reversed(range(tk // CUM_W)):
        blk = l1m[:, c * CUM_W:(c + 1) * CUM_W]
        after = jnp.dot(blk.astype(jnp.bfloat16), u_mat, preferred_element_type=jnp.float32) + carry
        parts.append(after)
        carry = carry + jnp.sum(blk, axis=-1, keepdims=True)
    after = jnp.concatenate(parts[::-1], axis=1) if len(parts) > 1 else parts[0]
    w = jnp.exp(z - sp + after)
    if mask is not None:
        w = jnp.where(mask, w, 0.0)
    out = jnp.dot(w.astype(jnp.bfloat16), v, preferred_element_type=jnp.float32)
    return out, carry


def _cum_matrix():
    j = lax.broadcasted_iota(jnp.int32, (CUM_W, CUM_W), 0)
    s = lax.broadcasted_iota(jnp.int32, (CUM_W, CUM_W), 1)
    return (j > s).astype(jnp.bfloat16)


def _prompt_attn_kernel(q_ref, k_ref, v_ref, bias_ref, g_ref, u_ref, o_ref, *, tq, tk):
    qi = pl.program_id(1)
    q = q_ref[...]
    bias = bias_ref[0, 0:1, 0:1]
    u_mat = u_ref[...]
    scale = HEAD_DIM ** -0.5
    n_diag = tq // tk

    def logits(j):
        kt = k_ref[pl.ds(pl.multiple_of(j * tk, tk), tk), :]
        z = lax.dot_general(q, kt, (((1,), (1,)), ((), ())), preferred_element_type=jnp.float32)
        return z * scale + bias

    def values(j):
        return v_ref[pl.ds(pl.multiple_of(j * tk, tk), tk), :]

    acc = jnp.zeros((tq, HEAD_DIM), jnp.float32)
    carry = jnp.zeros((tq, 1), jnp.float32)
    row = lax.broadcasted_iota(jnp.int32, (tq, tk), 0)
    col = lax.broadcasted_iota(jnp.int32, (tq, tk), 1)
    for d in reversed(range(n_diag)):
        j = qi * n_diag + d
        mask = (col + d * tk) < row
        out, carry = _sb_tile(logits(j), carry, values(j), u_mat, mask)
        acc = acc + out

    def body(s, state):
        acc, carry = state
        j = qi * n_diag - 1 - s
        out, carry = _sb_tile(logits(j), carry, values(j), u_mat, None)
        return acc + out, carry

    acc, carry = lax.fori_loop(0, qi * n_diag, body, (acc, carry))
    o_ref[...] = _rms(acc, g_ref[0:1, :]).astype(o_ref.dtype)


def prompt_attention(q, k, v, sb_bias, sb_g, *, tq=512, tk=512):
    s, width = q.shape
    n_heads = width // HEAD_DIM
    bias = jnp.broadcast_to(sb_bias.astype(jnp.float32)[:, None, None], (n_heads, 8, 128))
    g = jnp.broadcast_to(sb_g.astype(jnp.float32)[None, :], (8, HEAD_DIM))
    return pl.pallas_call(
        functools.partial(_prompt_attn_kernel, tq=tq, tk=tk),
        grid=(n_heads, s // tq),
        in_specs=[pl.BlockSpec((tq, HEAD_DIM), lambda h, i: (i, h)),
                  pl.BlockSpec((s, HEAD_DIM), lambda h, i: (0, h)),
                  pl.BlockSpec((s, HEAD_DIM), lambda h, i: (0, h)),
                  pl.BlockSpec((1, 8, 128), lambda h, i: (h, 0, 0)),
                  pl.BlockSpec((8, HEAD_DIM), lambda h, i: (0, 0)),
                  pl.BlockSpec((CUM_W, CUM_W), lambda h, i: (0, 0))],
        out_specs=pl.BlockSpec((tq, HEAD_DIM), lambda h, i: (i, h)),
        out_shape=jax.ShapeDtypeStruct((s, width), jnp.bfloat16),
        compiler_params=pltpu.CompilerParams(dimension_semantics=("parallel", "arbitrary"),
                                             vmem_limit_bytes=VMEM_LIMIT_BYTES),
        name="prompt_attention",
    )(q, k, v, bias, g, _cum_matrix())


def _sample_attn_kernel(pt_ref, q_ref, knew_ref, vnew_ref, bias_ref, mask_ref, g_ref, u_ref, *rest,
                        n_pages_step, page, n_heads):
    kp_refs = rest[:n_pages_step]
    vp_refs = rest[n_pages_step:2 * n_pages_step]
    o_ref, kbf, vbf, acc_ref, carry_ref = rest[2 * n_pages_step:]
    c = pl.program_id(1)
    nc = pl.num_programs(1)
    q = q_ref[0]
    bias = bias_ref[...]
    u_mat = u_ref[...]
    scale = HEAD_DIM ** -0.5

    def tile(k_bf, v_bf, carry, mask):
        z = lax.dot_general(q, k_bf, (((1,), (1,)), ((), ())), preferred_element_type=jnp.float32)
        z = z * scale + bias
        return _sb_tile(z, carry, v_bf, u_mat, mask)

    @pl.when(c == 0)
    def _():
        carry0 = jnp.zeros((q.shape[0], 1), jnp.float32)
        out, carry = tile(knew_ref[0].astype(jnp.bfloat16), vnew_ref[0].astype(jnp.bfloat16), carry0,
                          mask_ref[...] > 0.5)
        acc_ref[...] = out
        carry_ref[...] = carry

    @pl.when(c > 0)
    def _():
        for i in range(n_pages_step):
            for h in range(n_heads):
                rows = pl.ds(h, page, stride=n_heads)
                cols = slice(h * HEAD_DIM, (h + 1) * HEAD_DIM)
                kbf[i * page:(i + 1) * page, cols] = kp_refs[i].at[0][rows, :].astype(jnp.bfloat16)
                vbf[i * page:(i + 1) * page, cols] = vp_refs[i].at[0][rows, :].astype(jnp.bfloat16)
        out, carry = tile(kbf[...], vbf[...], carry_ref[...], None)
        acc_ref[...] += out
        carry_ref[...] = carry

    @pl.when(c == nc - 1)
    def _():
        acc = acc_ref[...]
        nq = acc.shape[0] // n_heads
        for h in range(n_heads):
            o = acc[h * nq:(h + 1) * nq, h * HEAD_DIM:(h + 1) * HEAD_DIM]
            o_ref[0, :, h * HEAD_DIM:(h + 1) * HEAD_DIM] = _rms(o, g_ref[0:1, :]).astype(o_ref.dtype)


def sample_attention(q, k_new, v_new, cache_k, cache_v, page_table, sb_bias, sb_g):
    b, ds, width = q.shape
    n_heads = width // HEAD_DIM
    page = cache_k.shape[1] // n_heads
    n_pages = page_table.shape[1]
    pps = PAGES_PER_STEP
    assert n_pages % pps == 0 and (pps * page) % CUM_W == 0
    n_chunks = n_pages // pps
    rows = n_heads * ds
    qh = jnp.transpose(q.reshape(b, ds, n_heads, HEAD_DIM), (0, 2, 1, 3))
    eye = jnp.eye(n_heads, dtype=q.dtype)
    q_bd = (qh[:, :, :, None, :] * eye[None, :, None, :, None]).reshape(b, rows, width)
    pad = CUM_W - ds
    k_pad = jnp.pad(k_new, ((0, 0), (0, pad), (0, 0)))
    v_pad = jnp.pad(v_new, ((0, 0), (0, pad), (0, 0)))
    bias = jnp.repeat(sb_bias.astype(jnp.float32), ds)[:, None]
    qpos = jnp.tile(jnp.arange(ds), n_heads)
    mask = (jnp.arange(CUM_W)[None, :] < qpos[:, None]).astype(jnp.float32)
    g = jnp.broadcast_to(sb_g.astype(jnp.float32)[None, :], (8, HEAD_DIM))

    def page_map(i):
        def f(bi, c, pt):
            cc = jnp.maximum(c - 1, 0)
            return (pt[bi, n_pages - (cc + 1) * pps + i % pps], 0, 0)
        return f

    full = lambda shape: pl.BlockSpec(shape, lambda bi, c, pt: (0,) * len(shape))
    in_specs = [pl.BlockSpec((1, rows, width), lambda bi, c, pt: (bi, 0, 0)),
                pl.BlockSpec((1, CUM_W, width), lambda bi, c, pt: (bi, 0, 0)),
                pl.BlockSpec((1, CUM_W, width), lambda bi, c, pt: (bi, 0, 0)),
                full((rows, 1)), full((rows, CUM_W)), full((8, HEAD_DIM)), full((CUM_W, CUM_W))]
    in_specs += [pl.BlockSpec((1, page * n_heads, HEAD_DIM), page_map(i)) for i in range(2 * pps)]
    grid_spec = pltpu.PrefetchScalarGridSpec(
        num_scalar_prefetch=1, grid=(b, n_chunks + 1), in_specs=in_specs,
        out_specs=pl.BlockSpec((1, ds, width), lambda bi, c, pt: (bi, 0, 0)),
        scratch_shapes=[pltpu.VMEM((pps * page, width), jnp.bfloat16),
                        pltpu.VMEM((pps * page, width), jnp.bfloat16),
                        pltpu.VMEM((rows, width), jnp.float32),
                        pltpu.VMEM((rows, 1), jnp.float32)])
    return pl.pallas_call(
        functools.partial(_sample_attn_kernel, n_pages_step=pps, page=page, n_heads=n_heads),
        grid_spec=grid_spec,
        out_shape=jax.ShapeDtypeStruct((b, ds, width), jnp.bfloat16),
        compiler_params=pltpu.CompilerParams(dimension_semantics=("parallel", "arbitrary"),
                                             vmem_limit_bytes=VMEM_LIMIT_BYTES),
        name="sample_attention",
    )(page_table, q_bd, k_pad, v_pad, bias, mask, g, _cum_matrix(), *([cache_k] * pps), *([cache_v] * pps))


def _router_kernel(h_ref, wr_ref, eb_ref, idx_ref, w_ref):
    h_hi, h_lo = _split_bf16(h_ref[...])
    w_hi, w_lo = _split_bf16(wr_ref[...])
    dot = lambda a, b: jnp.dot(a, b, preferred_element_type=jnp.float32)
    logits = dot(h_hi, w_hi) + (dot(h_lo, w_hi) + dot(h_hi, w_lo))
    s = jax.nn.sigmoid(logits)
    sel = s + eb_ref[0:1, :]
    tm, ne = s.shape
    gsz = ne // N_GROUPS
    lane = lax.broadcasted_iota(jnp.int32, (tm, ne), 1)
    grp = lane // gsz
    neg = -jnp.inf

    def argmax_first(x):
        m = jnp.max(x, axis=-1, keepdims=True)
        i = jnp.min(jnp.where(x == m, lane, ne), axis=-1, keepdims=True)
        return m, i

    glane = lax.broadcasted_iota(jnp.int32, (tm, N_GROUPS), 1)
    gscore = jnp.zeros((tm, N_GROUPS), jnp.float32)
    for g in range(N_GROUPS):
        vals = jnp.where(grp == g, sel, neg)
        m1, i1 = argmax_first(vals)
        m2 = jnp.max(jnp.where(lane == i1, neg, vals), axis=-1, keepdims=True)
        gscore = jnp.where(glane == g, m1 + m2, gscore)
    emask = jnp.zeros((tm, ne), jnp.bool_)
    cur = gscore
    for _ in range(TOPK_GROUPS):
        m = jnp.max(cur, axis=-1, keepdims=True)
        gi = jnp.min(jnp.where(cur == m, glane, N_GROUPS), axis=-1, keepdims=True)
        emask = jnp.logical_or(emask, grp == gi)
        cur = jnp.where(glane == gi, neg, cur)
    olane = lax.broadcasted_iota(jnp.int32, idx_ref.shape, 1)
    idx_out = jnp.zeros(idx_ref.shape, jnp.int32)
    w_out = jnp.zeros(w_ref.shape, jnp.float32)
    cur = jnp.where(emask, sel, neg)
    for k in range(TOP_K):
        _, ei = argmax_first(cur)
        hit = lane == ei
        wk = jnp.sum(jnp.where(hit, s, 0.0), axis=-1, keepdims=True)
        idx_out = jnp.where(olane == k, ei, idx_out)
        w_out = jnp.where(olane == k, wk, w_out)
        cur = jnp.where(hit, neg, cur)
    tot = jnp.sum(w_out, axis=-1, keepdims=True)
    idx_ref[...] = idx_out
    w_ref[...] = w_out / tot * ROUTED_SCALE


def route(h, w_router, e_bias, *, tm=256):
    t, d = h.shape
    assert t % tm == 0
    ne = w_router.shape[1]
    eb = jnp.broadcast_to(e_bias.astype(jnp.float32)[None, :], (8, ne))
    idx, w = pl.pallas_call(
        _router_kernel,
        grid=(t // tm,),
        in_specs=[pl.BlockSpec((tm, d), lambda i: (i, 0)),
                  pl.BlockSpec((d, ne), lambda i: (0, 0)),
                  pl.BlockSpec((8, ne), lambda i: (0, 0))],
        out_specs=[pl.BlockSpec((tm, 128), lambda i: (i, 0)),
                   pl.BlockSpec((tm, 128), lambda i: (i, 0))],
        out_shape=[jax.ShapeDtypeStruct((t, 128), jnp.int32),
                   jax.ShapeDtypeStruct((t, 128), jnp.float32)],
        compiler_params=pltpu.CompilerParams(dimension_semantics=("parallel",),
                                             vmem_limit_bytes=VMEM_LIMIT_BYTES),
        name="moe_router",
    )(h, w_router, eb)
    return idx[:, :TOP_K], w[:, :TOP_K]


def _row_copy(src_hbm, dst, sem, r, i):
    src = src_hbm.at[pl.ds(pl.multiple_of(r * LANE_ROWS, LANE_ROWS), LANE_ROWS)]
    return pltpu.make_async_copy(src, dst.at[pl.ds(pl.multiple_of(i * LANE_ROWS, LANE_ROWS), LANE_ROWS)], sem)


def _wait_rows(src_hbm, dst, sem):
    pltpu.make_async_copy(src_hbm.at[pl.ds(0, dst.shape[0])], dst, sem).wait()


def _start_run(src_hbm, idx_ref, idx_off, dst, dst_off, cnt, sem):
    def pair(p, _):
        for q in range(2):
            i = 2 * p + q
            _row_copy(src_hbm, dst, sem, idx_ref[0, 0, idx_off + i], dst_off + i).start(priority=q)
        return 0
    lax.fori_loop(0, cnt // 2, pair, 0)

    @pl.when(lax.rem(cnt, 2) == 1)
    def _():
        i = cnt - 1
        _row_copy(src_hbm, dst, sem, idx_ref[0, 0, idx_off + i], dst_off + i).start()


def _start_window_rows(src_hbm, lo_ref, hi_ref, off, dst, sem, n):
    _start_run(src_hbm, lo_ref, off, dst, 0, n - off, sem)
    _start_run(src_hbm, hi_ref, 0, dst, n - off, off, sem)


def _start_rows(src_hbm, idx_ref, dst, sem, n):
    _start_run(src_hbm, idx_ref, 0, dst, 0, n, sem)


def _expert_kernel(nused_ref, blk_e_ref, blk_src_ref, h_hbm, lo_ref, hi_ref, nlo_ref, nhi_ref,
                   wg_ref, wu_ref, wd_ref, y_ref, xbuf, sems, *, tm):
    i = pl.program_id(0)
    n_used = nused_ref[0]
    slot = lax.rem(i, 2)
    bf = jnp.bfloat16

    def ffn():
        _wait_rows(h_hbm, xbuf.at[slot], sems.at[slot])
        x = jnp.concatenate([p.astype(bf) for p in from_row_major(xbuf.at[slot], tm)], axis=1)
        g = jnp.dot(x, wg_ref[0].astype(bf), preferred_element_type=jnp.float32)
        u = jnp.dot(x, wu_ref[0].astype(bf), preferred_element_type=jnp.float32)
        a = (g * jax.nn.sigmoid(g) * u).astype(bf)
        to_row_major(y_ref, jnp.dot(a, wd_ref[0].astype(bf), preferred_element_type=jnp.float32))

    @pl.when(i == 0)
    def _():
        _start_window_rows(h_hbm, lo_ref, hi_ref, lax.rem(blk_src_ref[0], tm), xbuf.at[0], sems.at[0], tm)

    @pl.when(i + 1 < n_used)
    def _():
        _start_window_rows(h_hbm, nlo_ref, nhi_ref, lax.rem(blk_src_ref[i + 1], tm),
                           xbuf.at[1 - slot], sems.at[1 - slot], tm)
        ffn()

    @pl.when(i + 1 == n_used)
    def _():
        ffn()

    @pl.when(i >= n_used)
    def _():
        y_ref[...] = jnp.zeros_like(y_ref)


def expert_ffn(h, tok_sorted, blk_e, blk_src, n_used, w_gate, w_up, w_down, *, tm=MOE_TM):
    _, d, f = w_gate.shape
    assert d == LANE_ROWS * 128
    n_blocks = blk_e.shape[0]
    n_win = tok_sorted.shape[0] // tm
    win = tok_sorted.reshape(n_win, 1, tm)

    def wmap(step_off, win_off):
        def f_(i, nu, be, bs):
            b = jnp.minimum(i + step_off, n_blocks - 1)
            return (bs[b] // tm + win_off, 0, 0)
        return f_

    smem = lambda m: pl.BlockSpec((1, 1, tm), m, memory_space=pltpu.SMEM)
    wspec = lambda shape: pl.BlockSpec(shape, lambda i, nu, be, bs: (be[i], 0, 0))
    grid_spec = pltpu.PrefetchScalarGridSpec(
        num_scalar_prefetch=3, grid=(n_blocks,),
        in_specs=[pl.BlockSpec(memory_space=pl.ANY),
                  smem(wmap(0, 0)), smem(wmap(0, 1)), smem(wmap(1, 0)), smem(wmap(1, 1)),
                  wspec((1, d, f)), wspec((1, d, f)), wspec((1, f, d))],
        out_specs=pl.BlockSpec((tm * LANE_ROWS, 128), lambda i, nu, be, bs: (i, 0)),
        scratch_shapes=[pltpu.VMEM((2, tm * LANE_ROWS, 128), jnp.float32), pltpu.SemaphoreType.DMA((2,))])
    return pl.pallas_call(
        functools.partial(_expert_kernel, tm=tm),
        grid_spec=grid_spec,
        out_shape=jax.ShapeDtypeStruct((n_blocks * tm * LANE_ROWS, 128), jnp.float32),
        compiler_params=pltpu.CompilerParams(dimension_semantics=("arbitrary",),
                                             vmem_limit_bytes=VMEM_LIMIT_BYTES),
        name="moe_experts",
    )(n_used, blk_e, blk_src, h, win, win, win, win, w_gate, w_up, w_down)


def _combine_kernel(y_hbm, pos_ref, pos_next_ref, w_ref, sh_ref, x_ref, gate_ref, g_ref, o_ref,
                    ybuf, sems, *, tb):
    i = pl.program_id(0)
    n = pl.num_programs(0)
    slot = lax.rem(i, 2)
    rows = tb * TOP_K

    def combine():
        _wait_rows(y_hbm, ybuf.at[slot], sems.at[slot])
        w = w_ref[...]
        parts = None
        for k in range(TOP_K):
            yk = from_row_major(ybuf.at[slot], tb, k * tb)
            wk = w[:, k:k + 1]
            parts = [wk * p for p in yk] if parts is None else [a + wk * p for a, p in zip(parts, yk)]
        f = sh_ref[...] + jnp.concatenate(parts, axis=1)
        o_ref[...] = x_ref[...] + gate_ref[...] * _rms(f, g_ref[0:1, :])

    @pl.when(i == 0)
    def _():
        _start_rows(y_hbm, pos_ref, ybuf.at[0], sems.at[0], rows)

    @pl.when(i + 1 < n)
    def _():
        _start_rows(y_hbm, pos_next_ref, ybuf.at[1 - slot], sems.at[1 - slot], rows)
        combine()

    @pl.when(i + 1 == n)
    def _():
        combine()


def moe_combine(y_sorted, pos, w, shared, x, gate, post_g, row0, *, tb=COMBINE_TB):
    t, d = x.shape
    assert d == LANE_ROWS * 128
    assert t % tb == 0 and row0 % tb == 0 and pos.shape[0] % tb == 0
    nb = t // tb
    b0 = row0 // tb
    nb_all = pos.shape[0] // tb
    pos3 = jnp.transpose(pos.reshape(nb_all, tb, TOP_K), (0, 2, 1)).reshape(nb_all, 1, tb * TOP_K)
    g = jnp.broadcast_to(post_g.astype(jnp.float32)[None, :], (8, d))
    row = lambda i: (i, 0)
    row_all = lambda i: (b0 + i, 0)
    return pl.pallas_call(
        functools.partial(_combine_kernel, tb=tb),
        grid=(nb,),
        in_specs=[pl.BlockSpec(memory_space=pl.ANY),
                  pl.BlockSpec((1, 1, tb * TOP_K), lambda i: (b0 + i, 0, 0), memory_space=pltpu.SMEM),
                  pl.BlockSpec((1, 1, tb * TOP_K), lambda i: (b0 + jnp.minimum(i + 1, nb - 1), 0, 0),
                               memory_space=pltpu.SMEM),
                  pl.BlockSpec((tb, TOP_K), row_all),
                  pl.BlockSpec((tb, d), row_all), pl.BlockSpec((tb, d), row), _mod_spec(gate, tb),
                  pl.BlockSpec((8, d), lambda i: (0, 0))],
        out_specs=pl.BlockSpec((tb, d), row),
        out_shape=jax.ShapeDtypeStruct((t, d), jnp.float32),
        scratch_shapes=[pltpu.VMEM((2, tb * TOP_K * LANE_ROWS, 128), jnp.float32), pltpu.SemaphoreType.DMA((2,))],
        compiler_params=pltpu.CompilerParams(dimension_semantics=("arbitrary",),
                                             vmem_limit_bytes=VMEM_LIMIT_BYTES),
        name="moe_combine",
    )(y_sorted, pos3, pos3, w, shared, x, gate, g)


def dispatch_plan(eidx, *, tm=MOE_TM):
    t = eidx.shape[0]
    n = t * TOP_K
    assert n % tm == 0
    i32 = jnp.int32
    flat_e = eidx.reshape(n).astype(i32)
    a = jnp.arange(n, dtype=i32)
    experts = jnp.arange(N_EXPERTS, dtype=i32)
    se, order = lax.sort((flat_e, a), num_keys=1, is_stable=True)
    counts = jnp.sum((flat_e[:, None] == experts[None, :]).astype(i32), axis=0)
    pcounts = (counts + tm - 1) // tm * tm
    start = jnp.cumsum(counts) - counts
    pend = jnp.cumsum(pcounts)
    pstart = pend - pcounts
    pick = lambda sel, table: jnp.sum(jnp.where(sel[:, None] == experts[None, :], table[None, :], 0), axis=1)
    dest = a + pick(se, pstart - start)
    _, pos = lax.sort((order, dest), num_keys=1)
    n_blocks = -(-(n + N_EXPERTS * (tm - 1)) // tm)
    blk_row = jnp.arange(n_blocks, dtype=i32) * tm
    blk_e = jnp.minimum(jnp.sum((pend[None, :] <= blk_row[:, None]).astype(i32), axis=1), N_EXPERTS - 1)
    blk_src = jnp.clip(pick(blk_e, start - pstart) + blk_row, 0, n)
    tok_sorted = jnp.pad(order // TOP_K, (0, 2 * tm))
    n_used = (pend[-1] // tm).astype(i32).reshape(1)
    return tok_sorted, pos.reshape(t, TOP_K), blk_e, blk_src, n_used


SOLVE_BASE = 8
HALO = 8


def _bmm(a, b):
    return jnp.einsum('bij,bjk->bik', a, b, preferred_element_type=jnp.float32)


def _bmm_nt(a, b):
    return jnp.einsum('bid,bjd->bij', a, b, preferred_element_type=jnp.float32)


def _bmm_hp(a, b):
    a_hi, a_lo = _split_bf16(a)
    b_hi, b_lo = _split_bf16(b)
    return _bmm(a_hi, b_hi) + (_bmm(a_hi, b_lo) + _bmm(a_lo, b_hi))


def _unit_lower_inverse(a, ri, ci):
    c = a.shape[-1]
    s = SOLVE_BASE
    n1 = jnp.where((ri // s) == (ci // s), -a, 0.0)
    p = jnp.where(ri == ci, 1.0, 0.0) + n1
    n2 = _bmm_hp(n1, n1)
    p = p + _bmm_hp(p, n2)
    p = p + _bmm_hp(p, _bmm_hp(n2, n2))
    while s < c:
        join = jnp.where(((ri // (2 * s)) == (ci // (2 * s))) & ((ri // s) != (ci // s)), a, 0.0)
        p = p - _bmm_hp(_bmm_hp(p, join), p)
        s *= 2
    return p


def _gdn_prep_kernel(x_ref, prev_ref, buf_ref, ab_ref, cw_ref, alog_ref, dtb_ref,
                     q_ref, k_ref, v_ref, beta_ref, g_ref, *, t):
    i = pl.program_id(1)
    x = x_ref[0]
    halo = jnp.where(i == 0, buf_ref[0], prev_ref[0])
    full = jnp.concatenate([halo, x], axis=0)
    acc = None
    for j in range(CONV_W):
        off = HALO - (CONV_W - 1) + j
        term = full[off:off + t, :] * cw_ref[j:j + 1, :]
        acc = term if acc is None else acc + term
    c = acc * jax.nn.sigmoid(acc)
    ab = ab_ref[0]
    for h in range(GDN_HEADS):
        lo, hi = h * HEAD_DIM, (h + 1) * HEAD_DIM
        qh = c[:, lo:hi]
        kh = c[:, GDN_WIDTH + lo:GDN_WIDTH + hi]
        q_ref[0, :, lo:hi] = qh * lax.rsqrt(jnp.sum(qh * qh, axis=-1, keepdims=True) + 1e-6) * (HEAD_DIM ** -0.5)
        k_ref[0, :, lo:hi] = kh * lax.rsqrt(jnp.sum(kh * kh, axis=-1, keepdims=True) + 1e-6)
        v_ref[0, :, lo:hi] = c[:, 2 * GDN_WIDTH + lo:2 * GDN_WIDTH + hi]
        beta = jax.nn.sigmoid(ab[:, h:h + 1])
        z = ab[:, GDN_HEADS + h:GDN_HEADS + h + 1] + dtb_ref[0:1, h:h + 1]
        sp = jnp.maximum(z, 0.0) + jnp.log1p(jnp.exp(-jnp.abs(z)))
        g = -jnp.exp(alog_ref[0:1, h:h + 1]) * sp
        beta_ref[0, :, lo:hi] = jnp.broadcast_to(beta, (t, HEAD_DIM))
        g_ref[0, :, lo:hi] = jnp.broadcast_to(g, (t, HEAD_DIM))


def gdn_prep(qkv, ab, conv_buf, conv_w, a_log, dt_bias):
    b, l, w3 = qkv.shape
    t = min(l, 256)
    assert l % t == 0 and t % HALO == 0
    buf8 = jnp.pad(conv_buf, ((0, 0), (HALO - (CONV_W - 1), 0), (0, 0)))
    cw8 = jnp.pad(conv_w, ((0, 8 - CONV_W), (0, 0)))
    row8 = lambda vec: jnp.broadcast_to(jnp.pad(vec.astype(jnp.float32), (0, 128 - vec.shape[0]))[None, :], (8, 128))
    tpb = t // HALO
    out = jax.ShapeDtypeStruct((b, l, GDN_WIDTH), jnp.float32)
    ospec = pl.BlockSpec((1, t, GDN_WIDTH), lambda s, i: (s, i, 0))
    return pl.pallas_call(
        functools.partial(_gdn_prep_kernel, t=t),
        grid=(b, l // t),
        in_specs=[pl.BlockSpec((1, t, w3), lambda s, i: (s, i, 0)),
                  pl.BlockSpec((1, HALO, w3), lambda s, i: (s, jnp.maximum(i * tpb - 1, 0), 0)),
                  pl.BlockSpec((1, HALO, w3), lambda s, i: (s, 0, 0)),
                  pl.BlockSpec((1, t, 2 * GDN_HEADS), lambda s, i: (s, i, 0)),
                  pl.BlockSpec((8, w3), lambda s, i: (0, 0)),
                  pl.BlockSpec((8, 128), lambda s, i: (0, 0)),
                  pl.BlockSpec((8, 128), lambda s, i: (0, 0))],
        out_specs=[ospec] * 5,
        out_shape=[out] * 5,
        compiler_params=pltpu.CompilerParams(dimension_semantics=("parallel", "parallel"),
                                             vmem_limit_bytes=VMEM_LIMIT_BYTES),
        name="gdn_prep",
    )(qkv, qkv, buf8, ab, cw8, row8(a_log), row8(dt_bias))


def _gdn_chunk_kernel(q_ref, k_ref, v_ref, beta_ref, g_ref, w_ref, uv_ref, att_ref, qd_ref, kd_ref, dec_ref,
                      *, n_sub):
    c = GDN_CHUNK
    n_heads = q_ref.shape[1] // HEAD_DIM
    ri = lax.broadcasted_iota(jnp.int32, (c, c), 0)
    ci = lax.broadcasted_iota(jnp.int32, (c, c), 1)
    bf = jnp.bfloat16

    def batched(ref):
        x = ref[...]
        return jnp.concatenate([x[:, h * HEAD_DIM:(h + 1) * HEAD_DIM].reshape(n_sub, c, HEAD_DIM)
                                for h in range(n_heads)], axis=0)

    q, k, v, beta, g = (batched(r) for r in (q_ref, k_ref, v_ref, beta_ref, g_ref))
    nb = n_heads * n_sub
    gc_row = jnp.sum(jnp.where(ri <= ci, g[:, :, :c], 0.0), axis=1, keepdims=True)
    gc_col = jnp.sum(jnp.where(ri == ci, jnp.broadcast_to(gc_row, (nb, c, c)), 0.0), axis=2, keepdims=True)
    diff = gc_col - gc_row
    e_incl = jnp.exp(jnp.where(ri >= ci, diff, -jnp.inf))
    e_strict = jnp.where(ri > ci, e_incl, 0.0)
    kb = k * beta
    kbf = k.astype(bf)
    a_mat = _bmm_nt(kb.astype(bf), kbf) * e_strict
    rhs = jnp.concatenate([kb * jnp.exp(gc_col), v * beta], axis=2)
    x = _bmm_hp(_unit_lower_inverse(a_mat, ri, ci), rhs)
    att = _bmm_nt(q.astype(bf), kbf) * e_incl
    gc_last = gc_col[:, c - 1:c, :]
    qd = q * jnp.exp(gc_col)
    kd = k * jnp.exp(gc_last - gc_col)
    dec = jnp.broadcast_to(jnp.exp(gc_last), (nb, 1, HEAD_DIM))
    att_ref[...] = att.reshape(n_heads, n_sub * c, c).astype(att_ref.dtype)
    for h in range(n_heads):
        cols = slice(h * HEAD_DIM, (h + 1) * HEAD_DIM)
        part = lambda a: a[h * n_sub:(h + 1) * n_sub].reshape(n_sub * a.shape[1], a.shape[2])
        w_ref[:, cols] = part(x[:, :, :HEAD_DIM]).astype(w_ref.dtype)
        uv_ref[:, cols] = part(x[:, :, HEAD_DIM:])
        qd_ref[:, cols] = part(qd).astype(qd_ref.dtype)
        kd_ref[:, cols] = part(kd).astype(kd_ref.dtype)
        dec_ref[:, h:h + 1, :] = dec[h * n_sub:(h + 1) * n_sub]


def gdn_chunk(q, k, v, beta, g, *, n_sub=2):
    r, w = q.shape
    c = GDN_CHUNK
    n_chunks = r // c
    assert r % c == 0 and n_chunks % n_sub == 0
    t = n_sub * c
    n_heads = w // HEAD_DIM
    row = pl.BlockSpec((t, w), lambda i: (i, 0))
    f32, bf = jnp.float32, jnp.bfloat16
    return pl.pallas_call(
        functools.partial(_gdn_chunk_kernel, n_sub=n_sub),
        grid=(n_chunks // n_sub,),
        in_specs=[row] * 5,
        out_specs=[row, row,
                   pl.BlockSpec((n_heads, t, c), lambda i: (0, i, 0)),
                   row, row,
                   pl.BlockSpec((n_sub, n_heads, HEAD_DIM), lambda i: (i, 0, 0))],
        out_shape=[jax.ShapeDtypeStruct((r, w), bf), jax.ShapeDtypeStruct((r, w), f32),
                   jax.ShapeDtypeStruct((n_heads, r, c), bf),
                   jax.ShapeDtypeStruct((r, w), bf), jax.ShapeDtypeStruct((r, w), bf),
                   jax.ShapeDtypeStruct((n_chunks, n_heads, HEAD_DIM), f32)],
        compiler_params=pltpu.CompilerParams(dimension_semantics=("parallel",),
                                             vmem_limit_bytes=VMEM_LIMIT_BYTES),
        name="gdn_chunk",
    )(q, k, v, beta, g)


def _gdn_scan_kernel(w_ref, uv_ref, att_ref, qd_ref, kd_ref, dec_ref, zg_ref, s0_ref, gg_ref,
                     o_ref, sout_ref, state, *, n_sub):
    i = pl.program_id(1)
    c = GDN_CHUNK
    bf = jnp.bfloat16
    n_heads = state.shape[0]

    @pl.when(i == 0)
    def _():
        state[...] = s0_ref[0].astype(jnp.float32)

    def heads(ref, rows):
        return jnp.stack([ref[rows, h * HEAD_DIM:(h + 1) * HEAD_DIM] for h in range(n_heads)], axis=0)

    def chunk(ci, _):
        rows = pl.ds(pl.multiple_of(ci * c, c), c)
        s = state[...]
        sb = s.astype(bf)
        u = heads(uv_ref, rows) - _bmm(heads(w_ref, rows), sb)
        ub = u.astype(bf)
        o = _bmm(heads(qd_ref, rows), sb) + _bmm(att_ref[:, rows, :], ub)
        upd = jnp.einsum('hck,hcv->hkv', heads(kd_ref, rows), ub, preferred_element_type=jnp.float32)
        dec = dec_ref[ci]
        state[...] = dec[:, None, :] * s + upd
        z = heads(zg_ref, rows)
        y = (_rms(o, gg_ref[0:1, :]) * (z * jax.nn.sigmoid(z))).astype(o_ref.dtype)
        for h in range(n_heads):
            o_ref[rows, h * HEAD_DIM:(h + 1) * HEAD_DIM] = y[h]
        return 0

    lax.fori_loop(0, n_sub, chunk, 0)

    @pl.when(i == pl.num_programs(1) - 1)
    def _():
        sout_ref[0] = state[...].astype(sout_ref.dtype)


def gdn_scan(w, uv, att, qd, kd, dec, zg, s0, gdn_g):
    r, width = w.shape
    b = s0.shape[0]
    l = r // b
    c = GDN_CHUNK
    n_chunks = l // c
    n_heads = width // HEAD_DIM
    n_sub = 8 if n_chunks % 8 == 0 else n_chunks
    t = n_sub * c
    steps = n_chunks // n_sub
    row = pl.BlockSpec((t, width), lambda s, i: (s * steps + i, 0))
    g8 = jnp.broadcast_to(gdn_g.astype(jnp.float32)[None, :], (8, HEAD_DIM))
    state_spec = pl.BlockSpec((1, n_heads, HEAD_DIM, HEAD_DIM), lambda s, i: (s, 0, 0, 0))
    return pl.pallas_call(
        functools.partial(_gdn_scan_kernel, n_sub=n_sub),
        grid=(b, steps),
        in_specs=[row, row,
                  pl.BlockSpec((n_heads, t, c), lambda s, i: (0, s * steps + i, 0)),
                  row, row,
                  pl.BlockSpec((n_sub, n_heads, HEAD_DIM), lambda s, i: (s * steps + i, 0, 0)),
                  row, state_spec,
                  pl.BlockSpec((8, HEAD_DIM), lambda s, i: (0, 0))],
        out_specs=[row, state_spec],
        out_shape=[jax.ShapeDtypeStruct((r, width), jnp.bfloat16),
                   jax.ShapeDtypeStruct(s0.shape, s0.dtype)],
        scratch_shapes=[pltpu.VMEM((n_heads, HEAD_DIM, HEAD_DIM), jnp.float32)],
        compiler_params=pltpu.CompilerParams(dimension_semantics=("parallel", "arbitrary"),
                                             vmem_limit_bytes=VMEM_LIMIT_BYTES),
        name="gdn_scan",
    )(w, uv, att, qd, kd, dec, zg, s0, g8)


def gdn_mixer(qkv, zg, ab, conv_buf, s0, conv_w, a_log, dt_bias, gdn_g):
    b, l, _ = qkv.shape
    c = GDN_CHUNK
    q, k, v, beta, g = gdn_prep(qkv, ab, conv_buf, conv_w, a_log, dt_bias)
    lp = -(-l // c) * c
    rows = lambda a: jnp.pad(a, ((0, 0), (0, lp - l), (0, 0))).reshape(b * lp, GDN_WIDTH)
    w, uv, att, qd, kd, dec = gdn_chunk(*(rows(a) for a in (q, k, v, beta, g)))
    o, s_new = gdn_scan(w, uv, att, qd, kd, dec, rows(zg), s0, gdn_g)
    keep = CONV_W - 1
    new_buf = qkv[:, l - keep:] if l >= keep else jnp.concatenate([conv_buf, qkv], axis=1)[:, -keep:]
    return o.reshape(b, lp, GDN_WIDTH)[:, :l].reshape(b * l, GDN_WIDTH), new_buf, s_new


def _expand(mod_rows, reps):
    return mod_rows if reps is None else jnp.repeat(mod_rows, reps, axis=0)


def kernel(x_prompt, x_sample, cache_sb_k, cache_sb_v, state_conv, state_gdn, page_table, c_prompt, c_sample,
           w_ada, b_ada, pre_mix_g, post_mix_g, pre_ffn_g, post_ffn_g, w_in, conv_w, a_log, dt_bias, sb_bias,
           sb_out_g, gdn_out_g, w_out, w_router, e_bias, w_gate, w_up, w_down, ws_gate, ws_up, ws_down):
    l = 0
    bp, sp, d = x_prompt.shape
    bs, ss, _ = x_sample.shape
    assert bp == 1
    f32, bf16 = jnp.float32, jnp.bfloat16

    c_all = jnp.concatenate([c_prompt, c_sample], axis=0)
    c_pad = jnp.pad(c_all, ((0, (-c_all.shape[0]) % 8), (0, 0)))
    mod, = project([c_pad], w_ada[l], 0, 6 * d, (f32,), bias=b_ada[l], pre_silu=True)
    mod = mod[:bp + bs].reshape(bp + bs, 6, d)
    mod_p = [mod[:bp, i] for i in range(6)]
    mod_s = [_expand(mod[bp:, i], ss) for i in range(6)]

    groups = (
        dict(x=x_prompt.reshape(sp, d), mod=mod_p, batch=bp, length=sp),
        dict(x=x_sample.reshape(bs * ss, d), mod=mod_s, batch=bs, length=ss),
    )
    n_cache = cache_sb_k.shape[1]
    cache_k = cache_sb_k[l].reshape(n_cache, cache_sb_k.shape[2] * SB_HEADS, HEAD_DIM)
    cache_v = cache_sb_v[l].reshape(n_cache, cache_sb_v.shape[2] * SB_HEADS, HEAD_DIM)
    conv_bufs = (jnp.zeros((bp, CONV_W - 1, 3 * GDN_WIDTH), f32), state_conv[l])
    states = (jnp.zeros((bp, GDN_HEADS, HEAD_DIM, HEAD_DIM), state_gdn.dtype), state_gdn[l])

    results = []
    for gi, grp in enumerate(groups):
        x, (sh1, sc1, g1, sh2, sc2, g2) = grp['x'], grp['mod']
        b, length = grp['batch'], grp['length']
        h = prenorm(x, pre_mix_g[l], sc1, sh1, bf16)
        wi = w_in[l]
        q_sb, = project([h], wi, 0, SB_WIDTH, (bf16,))
        k_sb, k_bf = project([h], wi, SB_WIDTH, SB_WIDTH, (f32, bf16))
        v_sb, v_bf = project([h], wi, 2 * SB_WIDTH, SB_WIDTH, (f32, bf16))
        qkv, = project([h], wi, 3 * SB_WIDTH, 3 * GDN_WIDTH, (f32,))
        zg, = project([h], wi, 3 * SB_WIDTH + 3 * GDN_WIDTH, GDN_WIDTH, (f32,))
        ab, = project([h], wi, 3 * SB_WIDTH + 4 * GDN_WIDTH, 2 * GDN_HEADS, (f32,))
        if gi == 0:
            o_sb = prompt_attention(q_sb, k_bf, v_bf, sb_bias[l], sb_out_g[l])
        else:
            o_sb = sample_attention(q_sb.reshape(b, length, SB_WIDTH), k_sb.reshape(b, length, SB_WIDTH),
                                    v_sb.reshape(b, length, SB_WIDTH), cache_k, cache_v, page_table,
                                    sb_bias[l], sb_out_g[l]).reshape(b * length, SB_WIDTH)
        o_g, new_buf, s_new = gdn_mixer(qkv.reshape(b, length, -1), zg.reshape(b, length, -1),
                                        ab.reshape(b, length, -1), conv_bufs[gi], states[gi],
                                        conv_w[l], a_log[l], dt_bias[l], gdn_out_g[l])
        o, = project([o_sb, o_g], w_out[l], 0, d, (f32,))
        x1, h2, h2rm = postmix(x, o, g1, post_mix_g[l], pre_ffn_g[l], sc2, sh2)
        results.append(dict(x1=x1, h2=h2, h2rm=h2rm, g2=g2, k=k_sb.reshape(b, length, SB_HEADS, HEAD_DIM),
                            v=v_sb.reshape(b, length, SB_HEADS, HEAD_DIM), conv=new_buf, state=s_new))

    h2 = jnp.concatenate([r['h2'] for r in results], axis=0)
    h2rm = jnp.concatenate([r['h2rm'] for r in results], axis=0)
    eidx, ew = route(h2, w_router[l], e_bias[l])
    tok_sorted, pos, blk_e, blk_src, n_used = dispatch_plan(eidx)
    y_sorted = expert_ffn(h2rm, tok_sorted, blk_e, blk_src, n_used, w_gate[l], w_up[l], w_down[l])
    act = swiglu_up(h2, ws_gate[l], ws_up[l])
    shared, = project([act], ws_down[l], 0, d, (f32,))
    outs, t0 = [], 0
    for r in results:
        t1 = t0 + r['x1'].shape[0]
        outs.append(moe_combine(y_sorted, pos, ew, shared, r['x1'], r['g2'], post_ffn_g[l], t0))
        t0 = t1

    y_p = outs[0].reshape(bp, sp, d)
    y_s = outs[1].reshape(bs, ss, d)
    st = lambda t: t[None]
    rp, rs = results
    return (y_p, y_s, st(rp['k']), st(rp['v']), st(rp['conv']), st(rp['state']),
            st(rs['k']), st(rs['v']), st(rs['conv']), st(rs['state']))
```

```python
import functools

import jax
import jax.numpy as jnp
from jax import lax
from jax.experimental import pallas as pl
from jax.experimental.pallas import tpu as pltpu

D_MODEL = 2048
HEAD_DIM = 128
SB_HEADS = 8
GDN_HEADS = 8
SB_WIDTH = SB_HEADS * HEAD_DIM
GDN_WIDTH = GDN_HEADS * HEAD_DIM
CONV_W = 4
GDN_CHUNK = 64
N_EXPERTS = 64
TOP_K = 8
N_GROUPS = 8
TOPK_GROUPS = 4
ROUTED_SCALE = 2.5
NORM_EPS = 1e-6

VMEM_LIMIT_BYTES = 48 * 1024 * 1024
CUM_W = 256
PAGES_PER_STEP = 8
MOE_TM = 256
COMBINE_TB = 128
LANE_ROWS = D_MODEL // 128


def _pick_tile(n, candidates):
    for c in candidates:
        if n % c == 0:
            return c
    return n


def _split_bf16(x):
    hi = x.astype(jnp.bfloat16)
    lo = (x - hi.astype(jnp.float32)).astype(jnp.bfloat16)
    return hi, lo


def _rms(x, g_row):
    return x * lax.rsqrt(jnp.mean(x * x, axis=-1, keepdims=True) + NORM_EPS) * g_row


def to_row_major(ref, x):
    m = x.shape[0]
    for c in range(LANE_ROWS):
        ref[pl.ds(c, m, stride=LANE_ROWS), :] = x[:, c * 128:(c + 1) * 128]


def from_row_major(ref, m, row0=0):
    return [ref[pl.ds(row0 * LANE_ROWS + c, m, stride=LANE_ROWS), :] for c in range(LANE_ROWS)]


def _proj_kernel(*refs, n_lhs, k_splits, pre_silu, has_bias, n_out):
    lhs = refs[:n_lhs]
    w_ref = refs[n_lhs]
    pos = n_lhs + 1
    b_ref = refs[pos] if has_bias else None
    pos += int(has_bias)
    outs = refs[pos:pos + n_out]
    wbf = refs[pos + n_out]

    @pl.when(pl.program_id(1) == 0)
    def _():
        wbf[...] = w_ref[...].astype(jnp.bfloat16)

    acc = None
    for x_ref, (k0, k1) in zip(lhs, k_splits):
        x = x_ref[...]
        if pre_silu:
            x = x * jax.nn.sigmoid(x)
        part = jnp.dot(x.astype(jnp.bfloat16), wbf[k0:k1, :], preferred_element_type=jnp.float32)
        acc = part if acc is None else acc + part
    if has_bias:
        acc = acc + b_ref[0:1, :]
    for o in outs:
        o[...] = acc.astype(o.dtype)


def project(lhs, w, col0, ncols, out_dtypes, *, bias=None, pre_silu=False):
    m = lhs[0].shape[0]
    k = w.shape[0]
    k_splits, k0 = [], 0
    for x in lhs:
        k_splits.append((k0, k0 + x.shape[1]))
        k0 += x.shape[1]
    assert k0 == k and m % 8 == 0
    np_ = -(-ncols // 128) * 128
    if np_ != ncols:
        assert bias is None
        w = jnp.pad(lax.slice_in_dim(w, col0, col0 + ncols, axis=1), ((0, 0), (0, np_ - ncols)))
        col0 = 0
    tn = _pick_tile(np_, (1024, 512, 256, 128))
    assert col0 % tn == 0
    tm = _pick_tile(m, (512, 256, 128, 64, 32, 16, 8))
    in_specs = [pl.BlockSpec((tm, x.shape[1]), lambda j, i: (i, 0)) for x in lhs]
    in_specs.append(pl.BlockSpec((k, tn), lambda j, i: (0, col0 // tn + j)))
    args = list(lhs) + [w]
    if bias is not None:
        b = jnp.broadcast_to(bias.astype(jnp.float32)[None, :], (8, bias.shape[0]))
        in_specs.append(pl.BlockSpec((8, tn), lambda j, i: (0, col0 // tn + j)))
        args.append(b)
    outs = pl.pallas_call(
        functools.partial(_proj_kernel, n_lhs=len(lhs), k_splits=tuple(k_splits), pre_silu=pre_silu,
                          has_bias=bias is not None, n_out=len(out_dtypes)),
        grid=(np_ // tn, m // tm),
        in_specs=in_specs,
        out_specs=[pl.BlockSpec((tm, tn), lambda j, i: (i, j)) for _ in out_dtypes],
        out_shape=[jax.ShapeDtypeStruct((m, np_), dt) for dt in out_dtypes],
        scratch_shapes=[pltpu.VMEM((k, tn), jnp.bfloat16)],
        compiler_params=pltpu.CompilerParams(dimension_semantics=("parallel", "arbitrary"),
                                             vmem_limit_bytes=VMEM_LIMIT_BYTES),
        name="project",
    )(*args)
    if np_ != ncols:
        outs = [o[:, :ncols] for o in outs]
    return outs


def _swiglu_up_kernel(x_ref, wg_ref, wu_ref, o_ref, wgb, wub):
    @pl.when(pl.program_id(0) == 0)
    def _():
        wgb[...] = wg_ref[...].astype(jnp.bfloat16)
        wub[...] = wu_ref[...].astype(jnp.bfloat16)
    x = x_ref[...].astype(jnp.bfloat16)
    g = jnp.dot(x, wgb[...], preferred_element_type=jnp.float32)
    u = jnp.dot(x, wub[...], preferred_element_type=jnp.float32)
    o_ref[...] = (g * jax.nn.sigmoid(g) * u).astype(o_ref.dtype)


def swiglu_up(x, wg, wu):
    m, k = x.shape
    f = wg.shape[1]
    tm = _pick_tile(m, (768, 512, 256, 128, 64, 32, 16, 8))
    return pl.pallas_call(
        _swiglu_up_kernel,
        grid=(m // tm,),
        in_specs=[pl.BlockSpec((tm, k), lambda i: (i, 0)),
                  pl.BlockSpec((k, f), lambda i: (0, 0)),
                  pl.BlockSpec((k, f), lambda i: (0, 0))],
        out_specs=pl.BlockSpec((tm, f), lambda i: (i, 0)),
        out_shape=jax.ShapeDtypeStruct((m, f), jnp.bfloat16),
        scratch_shapes=[pltpu.VMEM((k, f), jnp.bfloat16), pltpu.VMEM((k, f), jnp.bfloat16)],
        compiler_params=pltpu.CompilerParams(dimension_semantics=("arbitrary",),
                                             vmem_limit_bytes=VMEM_LIMIT_BYTES),
        name="swiglu_up",
    )(x, wg, wu)


def _mod_spec(a, tm):
    if a.shape[0] == 1:
        return pl.BlockSpec((1, a.shape[1]), lambda i: (0, 0))
    return pl.BlockSpec((tm, a.shape[1]), lambda i: (i, 0))


def _prenorm_kernel(x_ref, g_ref, sc_ref, sh_ref, o_ref):
    y = _rms(x_ref[...], g_ref[0:1, :]) * (1.0 + sc_ref[...]) + sh_ref[...]
    o_ref[...] = y.astype(o_ref.dtype)


def prenorm(x, g, sc, sh, out_dtype):
    m, d = x.shape
    tm = _pick_tile(m, (512, 256, 128, 64, 32, 16, 8))
    g8 = jnp.broadcast_to(g.astype(jnp.float32)[None, :], (8, d))
    return pl.pallas_call(
        _prenorm_kernel,
        grid=(m // tm,),
        in_specs=[pl.BlockSpec((tm, d), lambda i: (i, 0)), pl.BlockSpec((8, d), lambda i: (0, 0)),
                  _mod_spec(sc, tm), _mod_spec(sh, tm)],
        out_specs=pl.BlockSpec((tm, d), lambda i: (i, 0)),
        out_shape=jax.ShapeDtypeStruct((m, d), out_dtype),
        compiler_params=pltpu.CompilerParams(dimension_semantics=("parallel",),
                                             vmem_limit_bytes=VMEM_LIMIT_BYTES),
        name="prenorm",
    )(x, g8, sc, sh)


def _postmix_kernel(xp_ref, op_ref, gp_ref, scp_ref, shp_ref, xs_ref, os_ref, gs_ref, scs_ref, shs_ref,
                    pg_ref, ng_ref, x1_ref, h2_ref, h2rm_ref, *, n_prompt_blocks):
    def body(x_ref, o_ref_in, gate_ref, sc_ref, sh_ref):
        x1 = x_ref[...] + gate_ref[...] * _rms(o_ref_in[...], pg_ref[0:1, :])
        x1_ref[...] = x1
        h2 = _rms(x1, ng_ref[0:1, :]) * (1.0 + sc_ref[...]) + sh_ref[...]
        h2_ref[...] = h2
        to_row_major(h2rm_ref, h2)

    i = pl.program_id(0)

    @pl.when(i < n_prompt_blocks)
    def _():
        body(xp_ref, op_ref, gp_ref, scp_ref, shp_ref)

    @pl.when(i >= n_prompt_blocks)
    def _():
        body(xs_ref, os_ref, gs_ref, scs_ref, shs_ref)


def postmix(prompt, sample, post_g, next_g):
    xp, xs = prompt[0], sample[0]
    (mp, d), ms = xp.shape, xs.shape[0]
    assert d == LANE_ROWS * 128
    tm = _pick_tile(ms, (256, 128, 64, 32, 16, 8))
    assert mp % tm == 0 and ms % tm == 0
    npb, nsb = mp // tm, ms // tm
    b8 = lambda v: jnp.broadcast_to(v.astype(jnp.float32)[None, :], (8, d))
    prow = pl.BlockSpec((tm, d), lambda i: (jnp.minimum(i, npb - 1), 0))
    srow = pl.BlockSpec((tm, d), lambda i: (jnp.maximum(i - npb, 0), 0))
    one = pl.BlockSpec((1, d), lambda i: (0, 0))
    c8 = pl.BlockSpec((8, d), lambda i: (0, 0))
    row = pl.BlockSpec((tm, d), lambda i: (i, 0))
    m = mp + ms
    return pl.pallas_call(
        functools.partial(_postmix_kernel, n_prompt_blocks=npb),
        grid=(npb + nsb,),
        in_specs=[prow, prow, one, one, one, srow, srow, srow, srow, srow, c8, c8],
        out_specs=[row, row, pl.BlockSpec((tm * LANE_ROWS, 128), lambda i: (i, 0))],
        out_shape=[jax.ShapeDtypeStruct((m, d), jnp.float32), jax.ShapeDtypeStruct((m, d), jnp.float32),
                   jax.ShapeDtypeStruct((m * LANE_ROWS, 128), jnp.float32)],
        compiler_params=pltpu.CompilerParams(dimension_semantics=("parallel",),
                                             vmem_limit_bytes=VMEM_LIMIT_BYTES),
        name="postmix",
    )(*prompt, *sample, b8(post_g), b8(next_g))


def _sb_tile(z, carry, v, u_mat, mask):
    sp = jnp.maximum(z, 0.0) + jnp.log(1.0 + jnp.exp(-jnp.abs(z)))
    l1m = -sp
    if mask is not None:
        l1m = jnp.where(mask, l1m, 0.0)
    tk = z.shape[1]
    parts = []
    for c in reversed(range(tk // CUM_W)):
        blk = l1m[:, c * CUM_W:(c + 1) * CUM_W]
        after = jnp.dot(blk.astype(jnp.bfloat16), u_mat, preferred_element_type=jnp.float32) + carry
        parts.append(after)
        carry = carry + jnp.sum(blk, axis=-1, keepdims=True)
    after = jnp.concatenate(parts[::-1], axis=1) if len(parts) > 1 else parts[0]
    w = jnp.exp(z - sp + after)
    if mask is not None:
        w = jnp.where(mask, w, 0.0)
    out = jnp.dot(w.astype(jnp.bfloat16), v, preferred_element_type=jnp.float32)
    return out, carry


def _cum_matrix():
    j = lax.broadcasted_iota(jnp.int32, (CUM_W, CUM_W), 0)
    s = lax.broadcasted_iota(jnp.int32, (CUM_W, CUM_W), 1)
    return (j > s).astype(jnp.bfloat16)


def _prompt_attn_kernel(q_ref, k_ref, v_ref, bias_ref, g_ref, u_ref, o_ref, *, tq, tk):
    qi = pl.program_id(1)
    q = q_ref[...]
    bias = bias_ref[0, 0:1, 0:1]
    u_mat = u_ref[...]
    scale = HEAD_DIM ** -0.5
    n_diag = tq // tk

    def logits(j):
        kt = k_ref[pl.ds(pl.multiple_of(j * tk, tk), tk), :]
        z = lax.dot_general(q, kt, (((1,), (1,)), ((), ())), preferred_element_type=jnp.float32)
        return z * scale + bias

    def values(j):
        return v_ref[pl.ds(pl.multiple_of(j * tk, tk), tk), :]

    acc = jnp.zeros((tq, HEAD_DIM), jnp.float32)
    carry = jnp.zeros((tq, 1), jnp.float32)
    row = lax.broadcasted_iota(jnp.int32, (tq, tk), 0)
    col = lax.broadcasted_iota(jnp.int32, (tq, tk), 1)
    for d in reversed(range(n_diag)):
        j = qi * n_diag + d
        mask = (col + d * tk) < row
        out, carry = _sb_tile(logits(j), carry, values(j), u_mat, mask)
        acc = acc + out

    def body(s, state):
        acc, carry = state
        j = qi * n_diag - 1 - s
        out, carry = _sb_tile(logits(j), carry, values(j), u_mat, None)
        return acc + out, carry

    acc, carry = lax.fori_loop(0, qi * n_diag, body, (acc, carry))
    o_ref[...] = _rms(acc, g_ref[0:1, :]).astype(o_ref.dtype)


def prompt_attention(q, k, v, sb_bias, sb_g, *, tq=512, tk=512):
    s, width = q.shape
    n_heads = width // HEAD_DIM
    bias = jnp.broadcast_to(sb_bias.astype(jnp.float32)[:, None, None], (n_heads, 8, 128))
    g = jnp.broadcast_to(sb_g.astype(jnp.float32)[None, :], (8, HEAD_DIM))
    return pl.pallas_call(
        functools.partial(_prompt_attn_kernel, tq=tq, tk=tk),
        grid=(n_heads, s // tq),
        in_specs=[pl.BlockSpec((tq, HEAD_DIM), lambda h, i: (i, h)),
                  pl.BlockSpec((s, HEAD_DIM), lambda h, i: (0, h)),
                  pl.BlockSpec((s, HEAD_DIM), lambda h, i: (0, h)),
                  pl.BlockSpec((1, 8, 128), lambda h, i: (h, 0, 0)),
                  pl.BlockSpec((8, HEAD_DIM), lambda h, i: (0, 0)),
                  pl.BlockSpec((CUM_W, CUM_W), lambda h, i: (0, 0))],
        out_specs=pl.BlockSpec((tq, HEAD_DIM), lambda h, i: (i, h)),
        out_shape=jax.ShapeDtypeStruct((s, width), jnp.bfloat16),
        compiler_params=pltpu.CompilerParams(dimension_semantics=("parallel", "arbitrary"),
                                             vmem_limit_bytes=VMEM_LIMIT_BYTES),
        name="prompt_attention",
    )(q, k, v, bias, g, _cum_matrix())


def _sample_attn_kernel(pt_ref, q_ref, knew_ref, vnew_ref, bias_ref, mask_ref, g_ref, u_ref, *rest,
                        n_pages_step, page, n_heads):
    kp_refs = rest[:n_pages_step]
    vp_refs = rest[n_pages_step:2 * n_pages_step]
    o_ref, kbf, vbf, acc_ref, carry_ref = rest[2 * n_pages_step:]
    c = pl.program_id(1)
    nc = pl.num_programs(1)
    q = q_ref[0]
    bias = bias_ref[...]
    u_mat = u_ref[...]
    scale = HEAD_DIM ** -0.5

    def tile(k_bf, v_bf, carry, mask):
        z = lax.dot_general(q, k_bf, (((1,), (1,)), ((), ())), preferred_element_type=jnp.float32)
        z = z * scale + bias
        return _sb_tile(z, carry, v_bf, u_mat, mask)

    @pl.when(c == 0)
    def _():
        carry0 = jnp.zeros((q.shape[0], 1), jnp.float32)
        out, carry = tile(knew_ref[0].astype(jnp.bfloat16), vnew_ref[0].astype(jnp.bfloat16), carry0,
                          mask_ref[...] > 0.5)
        acc_ref[...] = out
        carry_ref[...] = carry

    @pl.when(c > 0)
    def _():
        for i in range(n_pages_step):
            for h in range(n_heads):
                rows = pl.ds(h, page, stride=n_heads)
                cols = slice(h * HEAD_DIM, (h + 1) * HEAD_DIM)
                kbf[i * page:(i + 1) * page, cols] = kp_refs[i].at[0][rows, :].astype(jnp.bfloat16)
                vbf[i * page:(i + 1) * page, cols] = vp_refs[i].at[0][rows, :].astype(jnp.bfloat16)
        out, carry = tile(kbf[...], vbf[...], carry_ref[...], None)
        acc_ref[...] += out
        carry_ref[...] = carry

    @pl.when(c == nc - 1)
    def _():
        acc = acc_ref[...]
        nq = acc.shape[0] // n_heads
        for h in range(n_heads):
            o = acc[h * nq:(h + 1) * nq, h * HEAD_DIM:(h + 1) * HEAD_DIM]
            o_ref[0, :, h * HEAD_DIM:(h + 1) * HEAD_DIM] = _rms(o, g_ref[0:1, :]).astype(o_ref.dtype)


def sample_attention(q, k_new, v_new, cache_k, cache_v, page_table, sb_bias, sb_g):
    b, ds, width = q.shape
    n_heads = width // HEAD_DIM
    page = cache_k.shape[1] // n_heads
    n_pages = page_table.shape[1]
    pps = PAGES_PER_STEP
    assert n_pages % pps == 0 and (pps * page) % CUM_W == 0
    n_chunks = n_pages // pps
    rows = n_heads * ds
    qh = jnp.transpose(q.reshape(b, ds, n_heads, HEAD_DIM), (0, 2, 1, 3))
    eye = jnp.eye(n_heads, dtype=q.dtype)
    q_bd = (qh[:, :, :, None, :] * eye[None, :, None, :, None]).reshape(b, rows, width)
    pad = CUM_W - ds
    k_pad = jnp.pad(k_new, ((0, 0), (0, pad), (0, 0)))
    v_pad = jnp.pad(v_new, ((0, 0), (0, pad), (0, 0)))
    bias = jnp.repeat(sb_bias.astype(jnp.float32), ds)[:, None]
    qpos = jnp.tile(jnp.arange(ds), n_heads)
    mask = (jnp.arange(CUM_W)[None, :] < qpos[:, None]).astype(jnp.float32)
    g = jnp.broadcast_to(sb_g.astype(jnp.float32)[None, :], (8, HEAD_DIM))

    def page_map(i):
        def f(bi, c, pt):
            cc = jnp.maximum(c - 1, 0)
            return (pt[bi, n_pages - (cc + 1) * pps + i % pps], 0, 0)
        return f

    full = lambda shape: pl.BlockSpec(shape, lambda bi, c, pt: (0,) * len(shape))
    in_specs = [pl.BlockSpec((1, rows, width), lambda bi, c, pt: (bi, 0, 0)),
                pl.BlockSpec((1, CUM_W, width), lambda bi, c, pt: (bi, 0, 0)),
                pl.BlockSpec((1, CUM_W, width), lambda bi, c, pt: (bi, 0, 0)),
                full((rows, 1)), full((rows, CUM_W)), full((8, HEAD_DIM)), full((CUM_W, CUM_W))]
    in_specs += [pl.BlockSpec((1, page * n_heads, HEAD_DIM), page_map(i)) for i in range(2 * pps)]
    grid_spec = pltpu.PrefetchScalarGridSpec(
        num_scalar_prefetch=1, grid=(b, n_chunks + 1), in_specs=in_specs,
        out_specs=pl.BlockSpec((1, ds, width), lambda bi, c, pt: (bi, 0, 0)),
        scratch_shapes=[pltpu.VMEM((pps * page, width), jnp.bfloat16),
                        pltpu.VMEM((pps * page, width), jnp.bfloat16),
                        pltpu.VMEM((rows, width), jnp.float32),
                        pltpu.VMEM((rows, 1), jnp.float32)])
    return pl.pallas_call(
        functools.partial(_sample_attn_kernel, n_pages_step=pps, page=page, n_heads=n_heads),
        grid_spec=grid_spec,
        out_shape=jax.ShapeDtypeStruct((b, ds, width), jnp.bfloat16),
        compiler_params=pltpu.CompilerParams(dimension_semantics=("parallel", "arbitrary"),
                                             vmem_limit_bytes=VMEM_LIMIT_BYTES),
        name="sample_attention",
    )(page_table, q_bd, k_pad, v_pad, bias, mask, g, _cum_matrix(), *([cache_k] * pps), *([cache_v] * pps))


def _router_kernel(h_ref, wr_ref, eb_ref, idx_ref, w_ref):
    h_hi, h_lo = _split_bf16(h_ref[...])
    w_hi, w_lo = _split_bf16(wr_ref[...])
    dot = lambda a, b: jnp.dot(a, b, preferred_element_type=jnp.float32)
    logits = dot(h_hi, w_hi) + (dot(h_lo, w_hi) + dot(h_hi, w_lo))
    s = jax.nn.sigmoid(logits)
    sel = s + eb_ref[0:1, :]
    tm, ne = s.shape
    gsz = ne // N_GROUPS
    lane = lax.broadcasted_iota(jnp.int32, (tm, ne), 1)
    grp = lane // gsz
    neg = -jnp.inf

    def argmax_first(x):
        m = jnp.max(x, axis=-1, keepdims=True)
        i = jnp.min(jnp.where(x == m, lane, ne), axis=-1, keepdims=True)
        return m, i

    glane = lax.broadcasted_iota(jnp.int32, (tm, N_GROUPS), 1)
    gscore = jnp.zeros((tm, N_GROUPS), jnp.float32)
    for g in range(N_GROUPS):
        vals = jnp.where(grp == g, sel, neg)
        m1, i1 = argmax_first(vals)
        m2 = jnp.max(jnp.where(lane == i1, neg, vals), axis=-1, keepdims=True)
        gscore = jnp.where(glane == g, m1 + m2, gscore)
    emask = jnp.zeros((tm, ne), jnp.bool_)
    cur = gscore
    for _ in range(TOPK_GROUPS):
        m = jnp.max(cur, axis=-1, keepdims=True)
        gi = jnp.min(jnp.where(cur == m, glane, N_GROUPS), axis=-1, keepdims=True)
        emask = jnp.logical_or(emask, grp == gi)
        cur = jnp.where(glane == gi, neg, cur)
    olane = lax.broadcasted_iota(jnp.int32, idx_ref.shape, 1)
    idx_out = jnp.zeros(idx_ref.shape, jnp.int32)
    w_out = jnp.zeros(w_ref.shape, jnp.float32)
    cur = jnp.where(emask, sel, neg)
    for k in range(TOP_K):
        _, ei = argmax_first(cur)
        hit = lane == ei
        wk = jnp.sum(jnp.where(hit, s, 0.0), axis=-1, keepdims=True)
        idx_out = jnp.where(olane == k, ei, idx_out)
        w_out = jnp.where(olane == k, wk, w_out)
        cur = jnp.where(hit, neg, cur)
    tot = jnp.sum(w_out, axis=-1, keepdims=True)
    idx_ref[...] = idx_out
    w_ref[...] = w_out / tot * ROUTED_SCALE


def route(h, w_router, e_bias, *, tm=256):
    t, d = h.shape
    assert t % tm == 0
    ne = w_router.shape[1]
    eb = jnp.broadcast_to(e_bias.astype(jnp.float32)[None, :], (8, ne))
    idx, w = pl.pallas_call(
        _router_kernel,
        grid=(t // tm,),
        in_specs=[pl.BlockSpec((tm, d), lambda i: (i, 0)),
                  pl.BlockSpec((d, ne), lambda i: (0, 0)),
                  pl.BlockSpec((8, ne), lambda i: (0, 0))],
        out_specs=[pl.BlockSpec((tm, 128), lambda i: (i, 0)),
                   pl.BlockSpec((tm, 128), lambda i: (i, 0))],
        out_shape=[jax.ShapeDtypeStruct((t, 128), jnp.int32),
                   jax.ShapeDtypeStruct((t, 128), jnp.float32)],
        compiler_params=pltpu.CompilerParams(dimension_semantics=("parallel",),
                                             vmem_limit_bytes=VMEM_LIMIT_BYTES),
        name="moe_router",
    )(h, w_router, eb)
    return idx[:, :TOP_K], w[:, :TOP_K]


def _row_copy(src_hbm, dst, sem, r, i):
    src = src_hbm.at[pl.ds(pl.multiple_of(r * LANE_ROWS, LANE_ROWS), LANE_ROWS)]
    return pltpu.make_async_copy(src, dst.at[pl.ds(pl.multiple_of(i * LANE_ROWS, LANE_ROWS), LANE_ROWS)], sem)


def _wait_rows(src_hbm, dst, sem):
    pltpu.make_async_copy(src_hbm.at[pl.ds(0, dst.shape[0])], dst, sem).wait()


def _start_run(src_hbm, idx_ref, idx_off, dst, dst_off, cnt, sem):
    def pair(p, _):
        for q in range(2):
            i = 2 * p + q
            _row_copy(src_hbm, dst, sem, idx_ref[0, 0, idx_off + i], dst_off + i).start(priority=q)
        return 0
    lax.fori_loop(0, cnt // 2, pair, 0)

    @pl.when(lax.rem(cnt, 2) == 1)
    def _():
        i = cnt - 1
        _row_copy(src_hbm, dst, sem, idx_ref[0, 0, idx_off + i], dst_off + i).start()


def _start_window_rows(src_hbm, lo_ref, hi_ref, off, dst, sem, n):
    _start_run(src_hbm, lo_ref, off, dst, 0, n - off, sem)
    _start_run(src_hbm, hi_ref, 0, dst, n - off, off, sem)


def _start_rows(src_hbm, idx_ref, dst, sem, n):
    _start_run(src_hbm, idx_ref, 0, dst, 0, n, sem)


def _expert_kernel(nused_ref, blk_e_ref, blk_src_ref, h_hbm, lo_ref, hi_ref, nlo_ref, nhi_ref,
                   wg_ref, wu_ref, wd_ref, y_ref, xbuf, sems, *, tm):
    i = pl.program_id(0)
    n_used = nused_ref[0]
    slot = lax.rem(i, 2)
    bf = jnp.bfloat16

    def ffn():
        _wait_rows(h_hbm, xbuf.at[slot], sems.at[slot])
        x = jnp.concatenate([p.astype(bf) for p in from_row_major(xbuf.at[slot], tm)], axis=1)
        g = jnp.dot(x, wg_ref[0].astype(bf), preferred_element_type=jnp.float32)
        u = jnp.dot(x, wu_ref[0].astype(bf), preferred_element_type=jnp.float32)
        a = (g * jax.nn.sigmoid(g) * u).astype(bf)
        to_row_major(y_ref, jnp.dot(a, wd_ref[0].astype(bf), preferred_element_type=jnp.float32))

    @pl.when(i == 0)
    def _():
        _start_window_rows(h_hbm, lo_ref, hi_ref, lax.rem(blk_src_ref[0], tm), xbuf.at[0], sems.at[0], tm)

    @pl.when(i + 1 < n_used)
    def _():
        _start_window_rows(h_hbm, nlo_ref, nhi_ref, lax.rem(blk_src_ref[i + 1], tm),
                           xbuf.at[1 - slot], sems.at[1 - slot], tm)
        ffn()

    @pl.when(i + 1 == n_used)
    def _():
        ffn()

    @pl.when(i >= n_used)
    def _():
        y_ref[...] = jnp.zeros_like(y_ref)


def expert_ffn(h, tok_sorted, blk_e, blk_src, n_used, w_gate, w_up, w_down, *, tm=MOE_TM):
    _, d, f = w_gate.shape
    assert d == LANE_ROWS * 128
    n_blocks = blk_e.shape[0]
    n_win = tok_sorted.shape[0] // tm
    win = tok_sorted.reshape(n_win, 1, tm)

    def wmap(step_off, win_off):
        def f_(i, nu, be, bs):
            b = jnp.minimum(i + step_off, n_blocks - 1)
            return (bs[b] // tm + win_off, 0, 0)
        return f_

    smem = lambda m: pl.BlockSpec((1, 1, tm), m, memory_space=pltpu.SMEM)
    wspec = lambda shape: pl.BlockSpec(shape, lambda i, nu, be, bs: (be[i], 0, 0))
    grid_spec = pltpu.PrefetchScalarGridSpec(
        num_scalar_prefetch=3, grid=(n_blocks,),
        in_specs=[pl.BlockSpec(memory_space=pl.ANY),
                  smem(wmap(0, 0)), smem(wmap(0, 1)), smem(wmap(1, 0)), smem(wmap(1, 1)),
                  wspec((1, d, f)), wspec((1, d, f)), wspec((1, f, d))],
        out_specs=pl.BlockSpec((tm * LANE_ROWS, 128), lambda i, nu, be, bs: (i, 0)),
        scratch_shapes=[pltpu.VMEM((2, tm * LANE_ROWS, 128), jnp.float32), pltpu.SemaphoreType.DMA((2,))])
    return pl.pallas_call(
        functools.partial(_expert_kernel, tm=tm),
        grid_spec=grid_spec,
        out_shape=jax.ShapeDtypeStruct((n_blocks * tm * LANE_ROWS, 128), jnp.float32),
        compiler_params=pltpu.CompilerParams(dimension_semantics=("arbitrary",),
                                             vmem_limit_bytes=VMEM_LIMIT_BYTES),
        name="moe_experts",
    )(n_used, blk_e, blk_src, h, win, win, win, win, w_gate, w_up, w_down)


def _combine_kernel(y_hbm, pos_ref, pos_next_ref, w_ref, sh_ref, x_ref, gate_ref, g_ref, o_ref,
                    ybuf, sems, *, tb):
    i = pl.program_id(0)
    n = pl.num_programs(0)
    slot = lax.rem(i, 2)
    rows = tb * TOP_K

    def combine():
        _wait_rows(y_hbm, ybuf.at[slot], sems.at[slot])
        w = w_ref[...]
        parts = None
        for k in range(TOP_K):
            yk = from_row_major(ybuf.at[slot], tb, k * tb)
            wk = w[:, k:k + 1]
            parts = [wk * p for p in yk] if parts is None else [a + wk * p for a, p in zip(parts, yk)]
        f = sh_ref[...] + jnp.concatenate(parts, axis=1)
        o_ref[...] = x_ref[...] + gate_ref[...] * _rms(f, g_ref[0:1, :])

    @pl.when(i == 0)
    def _():
        _start_rows(y_hbm, pos_ref, ybuf.at[0], sems.at[0], rows)

    @pl.when(i + 1 < n)
    def _():
        _start_rows(y_hbm, pos_next_ref, ybuf.at[1 - slot], sems.at[1 - slot], rows)
        combine()

    @pl.when(i + 1 == n)
    def _():
        combine()


def moe_combine(y_sorted, pos, w, shared, x, t, gate, post_g, row0, *, tb=COMBINE_TB):
    d = x.shape[1]
    assert d == LANE_ROWS * 128
    assert t % tb == 0 and row0 % tb == 0 and pos.shape[0] % tb == 0
    nb = t // tb
    b0 = row0 // tb
    nb_all = pos.shape[0] // tb
    pos3 = jnp.transpose(pos.reshape(nb_all, tb, TOP_K), (0, 2, 1)).reshape(nb_all, 1, tb * TOP_K)
    g = jnp.broadcast_to(post_g.astype(jnp.float32)[None, :], (8, d))
    row = lambda i: (i, 0)
    row_all = lambda i: (b0 + i, 0)
    return pl.pallas_call(
        functools.partial(_combine_kernel, tb=tb),
        grid=(nb,),
        in_specs=[pl.BlockSpec(memory_space=pl.ANY),
                  pl.BlockSpec((1, 1, tb * TOP_K), lambda i: (b0 + i, 0, 0), memory_space=pltpu.SMEM),
                  pl.BlockSpec((1, 1, tb * TOP_K), lambda i: (b0 + jnp.minimum(i + 1, nb - 1), 0, 0),
                               memory_space=pltpu.SMEM),
                  pl.BlockSpec((tb, TOP_K), row_all),
                  pl.BlockSpec((tb, d), row_all), pl.BlockSpec((tb, d), row_all), _mod_spec(gate, tb),
                  pl.BlockSpec((8, d), lambda i: (0, 0))],
        out_specs=pl.BlockSpec((tb, d), row),
        out_shape=jax.ShapeDtypeStruct((t, d), jnp.float32),
        scratch_shapes=[pltpu.VMEM((2, tb * TOP_K * LANE_ROWS, 128), jnp.float32), pltpu.SemaphoreType.DMA((2,))],
        compiler_params=pltpu.CompilerParams(dimension_semantics=("arbitrary",),
                                             vmem_limit_bytes=VMEM_LIMIT_BYTES),
        name="moe_combine",
    )(y_sorted, pos3, pos3, w, shared, x, gate, g)


def dispatch_plan(eidx, *, tm=MOE_TM):
    t = eidx.shape[0]
    n = t * TOP_K
    assert n % tm == 0
    i32 = jnp.int32
    flat_e = eidx.reshape(n).astype(i32)
    a = jnp.arange(n, dtype=i32)
    experts = jnp.arange(N_EXPERTS, dtype=i32)
    se, order = lax.sort((flat_e, a), num_keys=1, is_stable=True)
    counts = jnp.sum((flat_e[:, None] == experts[None, :]).astype(i32), axis=0)
    pcounts = (counts + tm - 1) // tm * tm
    start = jnp.cumsum(counts) - counts
    pend = jnp.cumsum(pcounts)
    pstart = pend - pcounts
    pick = lambda sel, table: jnp.sum(jnp.where(sel[:, None] == experts[None, :], table[None, :], 0), axis=1)
    dest = a + pick(se, pstart - start)
    _, pos = lax.sort((order, dest), num_keys=1)
    n_blocks = -(-(n + N_EXPERTS * (tm - 1)) // tm)
    blk_row = jnp.arange(n_blocks, dtype=i32) * tm
    blk_e = jnp.minimum(jnp.sum((pend[None, :] <= blk_row[:, None]).astype(i32), axis=1), N_EXPERTS - 1)
    blk_src = jnp.clip(pick(blk_e, start - pstart) + blk_row, 0, n)
    tok_sorted = jnp.pad(order // TOP_K, (0, 2 * tm))
    n_used = (pend[-1] // tm).astype(i32).reshape(1)
    return tok_sorted, pos.reshape(t, TOP_K), blk_e, blk_src, n_used


SOLVE_BASE = 8
HALO = 8


def _bmm(a, b):
    return jnp.einsum('bij,bjk->bik', a, b, preferred_element_type=jnp.float32)


def _bmm_nt(a, b):
    return jnp.einsum('bid,bjd->bij', a, b, preferred_element_type=jnp.float32)


def _bmm_hp(a, b):
    a_hi, a_lo = _split_bf16(a)
    b_hi, b_lo = _split_bf16(b)
    return _bmm(a_hi, b_hi) + (_bmm(a_hi, b_lo) + _bmm(a_lo, b_hi))


def _unit_lower_inverse(a, ri, ci):
    c = a.shape[-1]
    s = SOLVE_BASE
    n1 = jnp.where((ri // s) == (ci // s), -a, 0.0)
    p = jnp.where(ri == ci, 1.0, 0.0) + n1
    n2 = _bmm_hp(n1, n1)
    p = p + _bmm_hp(p, n2)
    p = p + _bmm_hp(p, _bmm_hp(n2, n2))
    while s < c:
        join = jnp.where(((ri // (2 * s)) == (ci // (2 * s))) & ((ri // s) != (ci // s)), a, 0.0)
        p = p - _bmm_hp(_bmm_hp(p, join), p)
        s *= 2
    return p


def _gdn_prep_kernel(x_ref, prev_ref, buf_ref, ab_ref, cw_ref, alog_ref, dtb_ref,
                     q_ref, k_ref, v_ref, beta_ref, g_ref, *, t):
    i = pl.program_id(1)
    x = x_ref[0]
    halo = jnp.where(i == 0, buf_ref[0], prev_ref[0])
    full = jnp.concatenate([halo, x], axis=0)
    acc = None
    for j in range(CONV_W):
        off = HALO - (CONV_W - 1) + j
        term = full[off:off + t, :] * cw_ref[j:j + 1, :]
        acc = term if acc is None else acc + term
    c = acc * jax.nn.sigmoid(acc)
    ab = ab_ref[0]
    for h in range(GDN_HEADS):
        lo, hi = h * HEAD_DIM, (h + 1) * HEAD_DIM
        qh = c[:, lo:hi]
        kh = c[:, GDN_WIDTH + lo:GDN_WIDTH + hi]
        q_ref[0, :, lo:hi] = qh * lax.rsqrt(jnp.sum(qh * qh, axis=-1, keepdims=True) + 1e-6) * (HEAD_DIM ** -0.5)
        k_ref[0, :, lo:hi] = kh * lax.rsqrt(jnp.sum(kh * kh, axis=-1, keepdims=True) + 1e-6)
        v_ref[0, :, lo:hi] = c[:, 2 * GDN_WIDTH + lo:2 * GDN_WIDTH + hi]
        beta = jax.nn.sigmoid(ab[:, h:h + 1])
        z = ab[:, GDN_HEADS + h:GDN_HEADS + h + 1] + dtb_ref[0:1, h:h + 1]
        sp = jnp.maximum(z, 0.0) + jnp.log1p(jnp.exp(-jnp.abs(z)))
        g = -jnp.exp(alog_ref[0:1, h:h + 1]) * sp
        beta_ref[0, :, lo:hi] = jnp.broadcast_to(beta, (t, HEAD_DIM))
        g_ref[0, :, lo:hi] = jnp.broadcast_to(g, (t, HEAD_DIM))


def gdn_prep(qkv, ab, conv_buf, conv_w, a_log, dt_bias):
    b, l, w3 = qkv.shape
    t = min(l, 256)
    assert l % t == 0 and t % HALO == 0
    buf8 = jnp.pad(conv_buf, ((0, 0), (HALO - (CONV_W - 1), 0), (0, 0)))
    cw8 = jnp.pad(conv_w, ((0, 8 - CONV_W), (0, 0)))
    row8 = lambda vec: jnp.broadcast_to(jnp.pad(vec.astype(jnp.float32), (0, 128 - vec.shape[0]))[None, :], (8, 128))
    tpb = t // HALO
    out = jax.ShapeDtypeStruct((b, l, GDN_WIDTH), jnp.float32)
    ospec = pl.BlockSpec((1, t, GDN_WIDTH), lambda s, i: (s, i, 0))
    return pl.pallas_call(
        functools.partial(_gdn_prep_kernel, t=t),
        grid=(b, l // t),
        in_specs=[pl.BlockSpec((1, t, w3), lambda s, i: (s, i, 0)),
                  pl.BlockSpec((1, HALO, w3), lambda s, i: (s, jnp.maximum(i * tpb - 1, 0), 0)),
                  pl.BlockSpec((1, HALO, w3), lambda s, i: (s, 0, 0)),
                  pl.BlockSpec((1, t, 2 * GDN_HEADS), lambda s, i: (s, i, 0)),
                  pl.BlockSpec((8, w3), lambda s, i: (0, 0)),
                  pl.BlockSpec((8, 128), lambda s, i: (0, 0)),
                  pl.BlockSpec((8, 128), lambda s, i: (0, 0))],
        out_specs=[ospec] * 5,
        out_shape=[out] * 5,
        compiler_params=pltpu.CompilerParams(dimension_semantics=("parallel", "parallel"),
                                             vmem_limit_bytes=VMEM_LIMIT_BYTES),
        name="gdn_prep",
    )(qkv, qkv, buf8, ab, cw8, row8(a_log), row8(dt_bias))


def _gdn_chunk_kernel(q_ref, k_ref, v_ref, beta_ref, g_ref, w_ref, uv_ref, att_ref, qd_ref, kd_ref, dec_ref,
                      *, n_sub):
    c = GDN_CHUNK
    n_heads = q_ref.shape[1] // HEAD_DIM
    ri = lax.broadcasted_iota(jnp.int32, (c, c), 0)
    ci = lax.broadcasted_iota(jnp.int32, (c, c), 1)
    bf = jnp.bfloat16

    def batched(ref):
        x = ref[...]
        return jnp.concatenate([x[:, h * HEAD_DIM:(h + 1) * HEAD_DIM].reshape(n_sub, c, HEAD_DIM)
                                for h in range(n_heads)], axis=0)

    q, k, v, beta, g = (batched(r) for r in (q_ref, k_ref, v_ref, beta_ref, g_ref))
    nb = n_heads * n_sub
    gc_row = jnp.sum(jnp.where(ri <= ci, g[:, :, :c], 0.0), axis=1, keepdims=True)
    gc_col = jnp.sum(jnp.where(ri == ci, jnp.broadcast_to(gc_row, (nb, c, c)), 0.0), axis=2, keepdims=True)
    diff = gc_col - gc_row
    e_incl = jnp.exp(jnp.where(ri >= ci, diff, -jnp.inf))
    e_strict = jnp.where(ri > ci, e_incl, 0.0)
    kb = k * beta
    kbf = k.astype(bf)
    a_mat = _bmm_nt(kb.astype(bf), kbf) * e_strict
    rhs = jnp.concatenate([kb * jnp.exp(gc_col), v * beta], axis=2)
    x = _bmm_hp(_unit_lower_inverse(a_mat, ri, ci), rhs)
    att = _bmm_nt(q.astype(bf), kbf) * e_incl
    gc_last = gc_col[:, c - 1:c, :]
    qd = q * jnp.exp(gc_col)
    kd = k * jnp.exp(gc_last - gc_col)
    dec = jnp.broadcast_to(jnp.exp(gc_last), (nb, 1, HEAD_DIM))
    att_ref[...] = att.reshape(n_heads, n_sub * c, c).astype(att_ref.dtype)
    for h in range(n_heads):
        cols = slice(h * HEAD_DIM, (h + 1) * HEAD_DIM)
        part = lambda a: a[h * n_sub:(h + 1) * n_sub].reshape(n_sub * a.shape[1], a.shape[2])
        w_ref[:, cols] = part(x[:, :, :HEAD_DIM]).astype(w_ref.dtype)
        uv_ref[:, cols] = part(x[:, :, HEAD_DIM:])
        qd_ref[:, cols] = part(qd).astype(qd_ref.dtype)
        kd_ref[:, cols] = part(kd).astype(kd_ref.dtype)
        dec_ref[:, h:h + 1, :] = dec[h * n_sub:(h + 1) * n_sub]


def gdn_chunk(q, k, v, beta, g, *, n_sub=4):
    r, w = q.shape
    c = GDN_CHUNK
    n_chunks = r // c
    assert r % c == 0 and n_chunks % n_sub == 0
    t = n_sub * c
    n_heads = w // HEAD_DIM
    row = pl.BlockSpec((t, w), lambda i: (i, 0))
    f32, bf = jnp.float32, jnp.bfloat16
    return pl.pallas_call(
        functools.partial(_gdn_chunk_kernel, n_sub=n_sub),
        grid=(n_chunks // n_sub,),
        in_specs=[row] * 5,
        out_specs=[row, row,
                   pl.BlockSpec((n_heads, t, c), lambda i: (0, i, 0)),
                   row, row,
                   pl.BlockSpec((n_sub, n_heads, HEAD_DIM), lambda i: (i, 0, 0))],
        out_shape=[jax.ShapeDtypeStruct((r, w), bf), jax.ShapeDtypeStruct((r, w), f32),
                   jax.ShapeDtypeStruct((n_heads, r, c), bf),
                   jax.ShapeDtypeStruct((r, w), bf), jax.ShapeDtypeStruct((r, w), bf),
                   jax.ShapeDtypeStruct((n_chunks, n_heads, HEAD_DIM), f32)],
        compiler_params=pltpu.CompilerParams(dimension_semantics=("parallel",),
                                             vmem_limit_bytes=VMEM_LIMIT_BYTES),
        name="gdn_chunk",
    )(q, k, v, beta, g)


def _gdn_scan_kernel(w_ref, uv_ref, att_ref, qd_ref, kd_ref, dec_ref, zg_ref, s0_ref, gg_ref,
                     o_ref, sout_ref, state, *, n_sub):
    i = pl.program_id(1)
    c = GDN_CHUNK
    bf = jnp.bfloat16
    n_heads = state.shape[0]

    @pl.when(i == 0)
    def _():
        state[...] = s0_ref[0].astype(jnp.float32)

    def heads(ref, rows):
        return jnp.stack([ref[rows, h * HEAD_DIM:(h + 1) * HEAD_DIM] for h in range(n_heads)], axis=0)

    def chunk(ci, _):
        rows = pl.ds(pl.multiple_of(ci * c, c), c)
        s = state[...]
        sb = s.astype(bf)
        u = heads(uv_ref, rows) - _bmm(heads(w_ref, rows), sb)
        ub = u.astype(bf)
        o = _bmm(heads(qd_ref, rows), sb) + _bmm(att_ref[:, rows, :], ub)
        upd = jnp.einsum('hck,hcv->hkv', heads(kd_ref, rows), ub, preferred_element_type=jnp.float32)
        dec = dec_ref[ci]
        state[...] = dec[:, None, :] * s + upd
        z = heads(zg_ref, rows)
        y = (_rms(o, gg_ref[0:1, :]) * (z * jax.nn.sigmoid(z))).astype(o_ref.dtype)
        for h in range(n_heads):
            o_ref[rows, h * HEAD_DIM:(h + 1) * HEAD_DIM] = y[h]
        return 0

    lax.fori_loop(0, n_sub, chunk, 0)

    @pl.when(i == pl.num_programs(1) - 1)
    def _():
        sout_ref[0] = state[...].astype(sout_ref.dtype)


def gdn_scan(w, uv, att, qd, kd, dec, zg, s0, gdn_g):
    r, width = w.shape
    b = s0.shape[0]
    l = r // b
    c = GDN_CHUNK
    n_chunks = l // c
    n_heads = width // HEAD_DIM
    n_sub = 8 if n_chunks % 8 == 0 else n_chunks
    t = n_sub * c
    steps = n_chunks // n_sub
    row = pl.BlockSpec((t, width), lambda s, i: (s * steps + i, 0))
    g8 = jnp.broadcast_to(gdn_g.astype(jnp.float32)[None, :], (8, HEAD_DIM))
    state_spec = pl.BlockSpec((1, n_heads, HEAD_DIM, HEAD_DIM), lambda s, i: (s, 0, 0, 0))
    return pl.pallas_call(
        functools.partial(_gdn_scan_kernel, n_sub=n_sub),
        grid=(b, steps),
        in_specs=[row, row,
                  pl.BlockSpec((n_heads, t, c), lambda s, i: (0, s * steps + i, 0)),
                  row, row,
                  pl.BlockSpec((n_sub, n_heads, HEAD_DIM), lambda s, i: (s * steps + i, 0, 0)),
                  row, state_spec,
                  pl.BlockSpec((8, HEAD_DIM), lambda s, i: (0, 0))],
        out_specs=[row, state_spec],
        out_shape=[jax.ShapeDtypeStruct((r, width), jnp.bfloat16),
                   jax.ShapeDtypeStruct(s0.shape, s0.dtype)],
        scratch_shapes=[pltpu.VMEM((n_heads, HEAD_DIM, HEAD_DIM), jnp.float32)],
        compiler_params=pltpu.CompilerParams(dimension_semantics=("parallel", "arbitrary"),
                                             vmem_limit_bytes=VMEM_LIMIT_BYTES),
        name="gdn_scan",
    )(w, uv, att, qd, kd, dec, zg, s0, g8)


def gdn_mixer(qkv, zg, ab, conv_buf, s0, conv_w, a_log, dt_bias, gdn_g):
    b, l, _ = qkv.shape
    c = GDN_CHUNK
    q, k, v, beta, g = gdn_prep(qkv, ab, conv_buf, conv_w, a_log, dt_bias)
    lp = -(-l // c) * c
    rows = lambda a: jnp.pad(a, ((0, 0), (0, lp - l), (0, 0))).reshape(b * lp, GDN_WIDTH)
    w, uv, att, qd, kd, dec = gdn_chunk(*(rows(a) for a in (q, k, v, beta, g)))
    o, s_new = gdn_scan(w, uv, att, qd, kd, dec, rows(zg), s0, gdn_g)
    keep = CONV_W - 1
    new_buf = qkv[:, l - keep:] if l >= keep else jnp.concatenate([conv_buf, qkv], axis=1)[:, -keep:]
    return o.reshape(b, lp, GDN_WIDTH)[:, :l].reshape(b * l, GDN_WIDTH), new_buf, s_new


def _expand(mod_rows, reps):
    return mod_rows if reps is None else jnp.repeat(mod_rows, reps, axis=0)


def kernel(x_prompt, x_sample, cache_sb_k, cache_sb_v, state_conv, state_gdn, page_table, c_prompt, c_sample,
           w_ada, b_ada, pre_mix_g, post_mix_g, pre_ffn_g, post_ffn_g, w_in, conv_w, a_log, dt_bias, sb_bias,
           sb_out_g, gdn_out_g, w_out, w_router, e_bias, w_gate, w_up, w_down, ws_gate, ws_up, ws_down):
    l = 0
    bp, sp, d = x_prompt.shape
    bs, ss, _ = x_sample.shape
    assert bp == 1
    f32, bf16 = jnp.float32, jnp.bfloat16

    c_all = jnp.concatenate([c_prompt, c_sample], axis=0)
    c_pad = jnp.pad(c_all, ((0, (-c_all.shape[0]) % 8), (0, 0)))
    mod, = project([c_pad], w_ada[l], 0, 6 * d, (f32,), bias=b_ada[l], pre_silu=True)
    mod = mod[:bp + bs].reshape(bp + bs, 6, d)
    mod_p = [mod[:bp, i] for i in range(6)]
    mod_s = [_expand(mod[bp:, i], ss) for i in range(6)]

    groups = (
        dict(x=x_prompt.reshape(sp, d), mod=mod_p, batch=bp, length=sp),
        dict(x=x_sample.reshape(bs * ss, d), mod=mod_s, batch=bs, length=ss),
    )
    n_cache = cache_sb_k.shape[1]
    cache_k = cache_sb_k[l].reshape(n_cache, cache_sb_k.shape[2] * SB_HEADS, HEAD_DIM)
    cache_v = cache_sb_v[l].reshape(n_cache, cache_sb_v.shape[2] * SB_HEADS, HEAD_DIM)
    conv_bufs = (jnp.zeros((bp, CONV_W - 1, 3 * GDN_WIDTH), f32), state_conv[l])
    states = (jnp.zeros((bp, GDN_HEADS, HEAD_DIM, HEAD_DIM), state_gdn.dtype), state_gdn[l])

    results = []
    for gi, grp in enumerate(groups):
        x, (sh1, sc1, g1, sh2, sc2, g2) = grp['x'], grp['mod']
        b, length = grp['batch'], grp['length']
        h = prenorm(x, pre_mix_g[l], sc1, sh1, bf16)
        wi = w_in[l]
        q_sb, = project([h], wi, 0, SB_WIDTH, (bf16,))
        k_sb, k_bf = project([h], wi, SB_WIDTH, SB_WIDTH, (f32, bf16))
        v_sb, v_bf = project([h], wi, 2 * SB_WIDTH, SB_WIDTH, (f32, bf16))
        qkv, = project([h], wi, 3 * SB_WIDTH, 3 * GDN_WIDTH, (f32,))
        zg, = project([h], wi, 3 * SB_WIDTH + 3 * GDN_WIDTH, GDN_WIDTH, (f32,))
        ab, = project([h], wi, 3 * SB_WIDTH + 4 * GDN_WIDTH, 2 * GDN_HEADS, (f32,))
        if gi == 0:
            o_sb = prompt_attention(q_sb, k_bf, v_bf, sb_bias[l], sb_out_g[l])
        else:
            o_sb = sample_attention(q_sb.reshape(b, length, SB_WIDTH), k_sb.reshape(b, length, SB_WIDTH),
                                    v_sb.reshape(b, length, SB_WIDTH), cache_k, cache_v, page_table,
                                    sb_bias[l], sb_out_g[l]).reshape(b * length, SB_WIDTH)
        o_g, new_buf, s_new = gdn_mixer(qkv.reshape(b, length, -1), zg.reshape(b, length, -1),
                                        ab.reshape(b, length, -1), conv_bufs[gi], states[gi],
                                        conv_w[l], a_log[l], dt_bias[l], gdn_out_g[l])
        o, = project([o_sb, o_g], w_out[l], 0, d, (f32,))
        results.append(dict(post=(x, o, g1, sc2, sh2), g2=g2, k=k_sb.reshape(b, length, SB_HEADS, HEAD_DIM),
                            v=v_sb.reshape(b, length, SB_HEADS, HEAD_DIM), conv=new_buf, state=s_new))

    x1, h2, h2rm = postmix(results[0]['post'], results[1]['post'], post_mix_g[l], pre_ffn_g[l])
    eidx, ew = route(h2, w_router[l], e_bias[l])
    tok_sorted, pos, blk_e, blk_src, n_used = dispatch_plan(eidx)
    y_sorted = expert_ffn(h2rm, tok_sorted, blk_e, blk_src, n_used, w_gate[l], w_up[l], w_down[l])
    act = swiglu_up(h2, ws_gate[l], ws_up[l])
    shared, = project([act], ws_down[l], 0, d, (f32,))
    outs, t0 = [], 0
    for r in results:
        n_tok = r['post'][0].shape[0]
        outs.append(moe_combine(y_sorted, pos, ew, shared, x1, n_tok, r['g2'], post_ffn_g[l], t0))
        t0 += n_tok

    y_p = outs[0].reshape(bp, sp, d)
    y_s = outs[1].reshape(bs, ss, d)
    st = lambda t: t[None]
    rp, rs = results
    return (y_p, y_s, st(rp['k']), st(rp['v']), st(rp['conv']), st(rp['state']),
            st(rs['k']), st(rs['v']), st(rs['conv']), st(rs['state']))
```
